```python
import math
import jax, jax.numpy as jnp
from jax import lax
import numpy as np

D_MODEL = 2048
BATCH = 4
SEQ = 4096
DEPTH = 2

GDN_QK_HEADS = 8
GDN_V_HEADS = 16
GDN_HEAD_DIM = 128
GDN_CONV = 4
GDN_CHUNK = 64
SWA_Q_HEADS = 32
SWA_KV_HEADS = 4
SWA_HEAD_DIM = 64
SWA_WINDOW = 128
REL_BUCKETS = 32
REL_MAX_DIST = 128
D_FF_DENSE = 128 * ((8 * D_MODEL // 3 + 127) // 128)
N_EXPERTS = 8
TOP_K = 2
D_FF_EXPERT = 7 * D_MODEL // 2
MOE_BLOCK = 256
NORM_EPS = 1e-6

GDN_KEY_DIM = GDN_QK_HEADS * GDN_HEAD_DIM
GDN_VAL_DIM = GDN_V_HEADS * GDN_HEAD_DIM
GDN_CONV_DIM = 2 * GDN_KEY_DIM + GDN_VAL_DIM
SWA_Q_DIM = SWA_Q_HEADS * SWA_HEAD_DIM
SWA_KV_DIM = SWA_KV_HEADS * SWA_HEAD_DIM
IN_SIZES = (GDN_KEY_DIM, GDN_KEY_DIM, GDN_VAL_DIM, GDN_VAL_DIM, GDN_V_HEADS, GDN_V_HEADS,
            SWA_Q_DIM, SWA_KV_DIM, SWA_KV_DIM, D_MODEL, D_MODEL)
D_IN = sum(IN_SIZES)

kernel_name = "hybrid_gdn_swa_sink_moe_block"


def rmsnorm(x, g):
    x32 = x.astype(jnp.float32)
    y = x32 * lax.rsqrt(jnp.mean(x32 * x32, axis=-1, keepdims=True) + NORM_EPS)
    return (y * g.astype(jnp.float32)).astype(x.dtype)


def l2norm(t):
    return t * lax.rsqrt(jnp.sum(t * t, axis=-1, keepdims=True) + 1e-6)


def causal_conv_silu(u, w):
    K = w.shape[0]
    S = u.shape[1]
    up = jnp.pad(u, ((0, 0), (K - 1, 0), (0, 0)))
    y = up[:, 0:S] * w[0]
    for j in range(1, K):
        y = y + up[:, j:j + S] * w[j]
    return jax.nn.silu(y)


def gated_delta_rule(q, k, v, g, beta):
    B, S, H, dk = q.shape
    dv = v.shape[-1]
    C = GDN_CHUNK
    N = S // C

    def chunks(t):
        return t.reshape(B, N, C, H, -1).transpose(1, 0, 3, 2, 4)

    qc, kc, vc = chunks(q), chunks(k), chunks(v)
    gc = jnp.cumsum(chunks(g[..., None])[..., 0], axis=-1)
    bc = chunks(beta[..., None])
    causal = jnp.tril(jnp.ones((C, C), bool))
    strict = jnp.tril(jnp.ones((C, C), bool), -1)
    decay = jnp.exp(jnp.where(causal, gc[..., :, None] - gc[..., None, :], -jnp.inf))
    kb = kc * bc
    vb = vc * bc
    a_mat = jnp.where(strict, jnp.einsum('nbhcd,nbhsd->nbhcs', kb, kc) * decay, 0.0) + jnp.eye(C, dtype=q.dtype)
    rhs = jnp.concatenate([vb, kb * jnp.exp(gc)[..., None]], axis=-1)
    sol = lax.linalg.triangular_solve(a_mat, rhs, left_side=True, lower=True)
    u, w = sol[..., :dv], sol[..., dv:]
    qk = jnp.einsum('nbhcd,nbhsd->nbhcs', qc, kc) * decay
    q_dec = qc * jnp.exp(gc)[..., None]
    k_dec = kc * jnp.exp(gc[..., -1:] - gc)[..., None]
    g_last = jnp.exp(gc[..., -1])

    def step(state, xs):
        u_n, w_n, qk_n, q_n, k_n, gl_n = xs
        v_new = u_n - jnp.einsum('bhcd,bhde->bhce', w_n, state)
        o = jnp.einsum('bhcd,bhde->bhce', q_n, state) + jnp.einsum('bhcs,bhse->bhce', qk_n, v_new)
        state = state * gl_n[..., None, None] + jnp.einsum('bhcd,bhce->bhde', k_n, v_new)
        return state, o

    s0 = jnp.zeros((B, H, dk, dv), q.dtype)
    _, o = lax.scan(step, s0, (u, w, qk, q_dec, k_dec, g_last))
    return o.transpose(1, 0, 3, 2, 4).reshape(B, S, H, dv)


def t5_bucket(dist):
    max_exact = REL_BUCKETS // 2
    large = max_exact + (jnp.log(jnp.maximum(dist, 1).astype(jnp.float32) / max_exact)
                         / math.log(REL_MAX_DIST / max_exact) * (REL_BUCKETS - max_exact)).astype(jnp.int32)
    large = jnp.minimum(large, REL_BUCKETS - 1)
    return jnp.where(dist < max_exact, dist, large)


def sliding_window_sink_attention(q, k, v, sinks, rel_bias):
    B, S, Hq, dh = q.shape
    Hkv = k.shape[2]
    G = Hq // Hkv
    W = SWA_WINDOW
    NB = S // W
    qb = q.reshape(B, NB, W, Hkv, G, dh)

    def band(t):
        cur = t.reshape(B, NB, W, Hkv, dh)
        prev = jnp.pad(t, ((0, 0), (W, 0), (0, 0), (0, 0)))[:, :S].reshape(B, NB, W, Hkv, dh)
        return jnp.concatenate([prev, cur], axis=2)

    kb, vb = band(k), band(v)
    logits = jnp.einsum('bnqhgd,bnkhd->bhgnqk', qb, kb).astype(jnp.float32) * (dh ** -0.5)
    qi = jnp.arange(W)[:, None]
    ki = jnp.arange(2 * W)[None, :]
    dist = qi + W - ki
    in_window = (dist >= 0) & (dist < W)
    valid = in_window[None] & ((jnp.arange(NB)[:, None, None] > 0) | (ki >= W)[None])
    bias = rel_bias.astype(jnp.float32)[t5_bucket(jnp.clip(dist, 0, W - 1))]
    bias = bias.transpose(2, 0, 1).reshape(Hkv, G, 1, W, 2 * W)
    logits = jnp.where(valid, logits + bias, -jnp.inf)
    sink = sinks.astype(jnp.float32).reshape(Hkv, G, 1, 1, 1)
    m = jnp.maximum(jnp.max(logits, axis=-1, keepdims=True), sink)
    p = jnp.exp(logits - m)
    probs = p / (jnp.sum(p, axis=-1, keepdims=True) + jnp.exp(sink - m))
    out = jnp.einsum('bhgnqk,bnkhd->bnqhgd', probs.astype(v.dtype), vb)
    return out.reshape(B, S, Hq * dh)


def swiglu(h, w_gate, w_up, w_down):
    return (jax.nn.silu(h @ w_gate) * (h @ w_up)) @ w_down


def moe_swiglu(h, router_w, w_gate, w_up, w_down):
    B, S, D = h.shape
    T = B * S
    TK = T * TOP_K
    ht = h.reshape(T, D)
    logits = (ht @ router_w).astype(jnp.float32)
    top_logit, top_e = lax.top_k(logits, TOP_K)
    top_w = jax.nn.softmax(top_logit, axis=-1).astype(h.dtype)
    flat_e = top_e.reshape(TK).astype(jnp.int32)
    order = jnp.argsort(flat_e).astype(jnp.int32)
    sorted_e = flat_e[order]
    sorted_tok = order // TOP_K
    counts = jnp.bincount(flat_e, length=N_EXPERTS).astype(jnp.int32)
    padded = (counts + MOE_BLOCK - 1) // MOE_BLOCK * MOE_BLOCK
    start = jnp.cumsum(counts) - counts
    pad_end = jnp.cumsum(padded)
    pad_start = pad_end - padded
    dest = pad_start[sorted_e] + jnp.arange(TK, dtype=jnp.int32) - start[sorted_e]
    n_rows = TK + N_EXPERTS * MOE_BLOCK
    n_blocks = n_rows // MOE_BLOCK
    row_tok = jnp.full((n_rows,), T, jnp.int32).at[dest].set(sorted_tok)
    xg = jnp.concatenate([ht, jnp.zeros((1, D), ht.dtype)], axis=0)[row_tok]
    block_e = jnp.minimum(jnp.searchsorted(pad_end, jnp.arange(n_blocks, dtype=jnp.int32) * MOE_BLOCK,
                                           side='right'), N_EXPERTS - 1).astype(jnp.int32)

    def expert_block(args):
        xb, e = args
        return swiglu(xb, w_gate[e], w_up[e], w_down[e])

    yg = lax.map(expert_block, (xg.reshape(n_blocks, MOE_BLOCK, D), block_e)).reshape(n_rows, D)
    dest_tok = jnp.zeros((TK,), jnp.int32).at[order].set(dest)
    y = jnp.einsum('tkd,tk->td', yg[dest_tok].reshape(T, TOP_K, D), top_w)
    return y.reshape(B, S, D)


def setup_inputs(seed: int = 0) -> dict:
    key = jax.random.key(seed)
    ks = jax.random.split(key, 24)
    f32 = jnp.float32
    L = DEPTH
    Ld = (DEPTH + 1) // 2
    Lm = DEPTH // 2

    def nrm(k, shape, fan_in):
        return jax.random.normal(k, shape, f32) * (fan_in ** -0.5)

    def gain(k, shape):
        return 1.0 + 0.02 * jax.random.normal(k, shape, f32)

    dt = jnp.exp(jax.random.uniform(ks[5], (L, GDN_V_HEADS), f32, math.log(1e-3), math.log(1e-1)))
    return {
        "x": jax.random.normal(ks[0], (BATCH, SEQ, D_MODEL), f32),
        "ln_mix_g": gain(ks[1], (L, D_MODEL)),
        "w_in": nrm(ks[2], (L, D_MODEL, D_IN), D_MODEL),
        "conv_w": nrm(ks[3], (L, GDN_CONV, GDN_CONV_DIM), GDN_CONV),
        "a_log": jnp.log(jax.random.uniform(ks[4], (L, GDN_V_HEADS), f32, 1.0, 16.0)),
        "dt_bias": dt + jnp.log(-jnp.expm1(-dt)),
        "gdn_norm_g": gain(ks[6], (L, GDN_HEAD_DIM)),
        "sinks": jax.random.normal(ks[7], (L, SWA_Q_HEADS), f32),
        "rel_bias": 0.5 * jax.random.normal(ks[8], (REL_BUCKETS, SWA_Q_HEADS), f32),
        "w_proj_gdn": nrm(ks[9], (L, GDN_VAL_DIM, D_MODEL), GDN_VAL_DIM),
        "w_proj_swa": nrm(ks[10], (L, SWA_Q_DIM, D_MODEL), SWA_Q_DIM),
        "w_out": nrm(ks[11], (L, D_MODEL, D_MODEL), D_MODEL),
        "ln_ffn_g": gain(ks[12], (L, D_MODEL)),
        "ffn_w_gate": nrm(ks[13], (Ld, D_MODEL, D_FF_DENSE), D_MODEL),
        "ffn_w_up": nrm(ks[14], (Ld, D_MODEL, D_FF_DENSE), D_MODEL),
        "ffn_w_down": nrm(ks[15], (Ld, D_FF_DENSE, D_MODEL), D_FF_DENSE),
        "router_w": nrm(ks[16], (Lm, D_MODEL, N_EXPERTS), D_MODEL),
        "moe_w_gate": nrm(ks[17], (Lm, N_EXPERTS, D_MODEL, D_FF_EXPERT), D_MODEL),
        "moe_w_up": nrm(ks[18], (Lm, N_EXPERTS, D_MODEL, D_FF_EXPERT), D_MODEL),
        "moe_w_down": nrm(ks[19], (Lm, N_EXPERTS, D_FF_EXPERT, D_MODEL), D_FF_EXPERT),
        "ln_final_g": gain(ks[20], (D_MODEL,)),
    }


def reference(x, ln_mix_g, w_in, conv_w, a_log, dt_bias, gdn_norm_g, sinks, rel_bias,
              w_proj_gdn, w_proj_swa, w_out, ln_ffn_g, ffn_w_gate, ffn_w_up, ffn_w_down,
              router_w, moe_w_gate, moe_w_up, moe_w_down, ln_final_g):
    B, S, _ = x.shape
    f32 = jnp.float32
    splits = np.cumsum(IN_SIZES)[:-1].tolist()
    rep = GDN_V_HEADS // GDN_QK_HEADS
    for i in range(DEPTH):
        h = rmsnorm(x, ln_mix_g[i])
        proj = h @ w_in[i]
        q, k, v, z, b, a, sq, sk, sv, gate_a, gate_b = jnp.split(proj, splits, axis=-1)

        qkv = causal_conv_silu(jnp.concatenate([q, k, v], axis=-1), conv_w[i]).astype(f32)
        gq, gk, gv = jnp.split(qkv, [GDN_KEY_DIM, 2 * GDN_KEY_DIM], axis=-1)
        gq = jnp.repeat(l2norm(gq.reshape(B, S, GDN_QK_HEADS, GDN_HEAD_DIM)), rep, axis=2) * (GDN_HEAD_DIM ** -0.5)
        gk = jnp.repeat(l2norm(gk.reshape(B, S, GDN_QK_HEADS, GDN_HEAD_DIM)), rep, axis=2)
        gv = gv.reshape(B, S, GDN_V_HEADS, GDN_HEAD_DIM)
        beta = jax.nn.sigmoid(b.astype(f32))
        g = -jnp.exp(a_log[i].astype(f32)) * jax.nn.softplus(a.astype(f32) + dt_bias[i].astype(f32))
        o_a = gated_delta_rule(gq, gk, gv, g, beta)
        o_a = o_a * lax.rsqrt(jnp.mean(o_a * o_a, axis=-1, keepdims=True) + NORM_EPS) * gdn_norm_g[i].astype(f32)
        o_a = o_a * jax.nn.silu(z.astype(f32).reshape(B, S, GDN_V_HEADS, GDN_HEAD_DIM))
        o_a = o_a.reshape(B, S, GDN_VAL_DIM).astype(x.dtype)

        o_b = sliding_window_sink_attention(sq.reshape(B, S, SWA_Q_HEADS, SWA_HEAD_DIM),
                                            sk.reshape(B, S, SWA_KV_HEADS, SWA_HEAD_DIM),
                                            sv.reshape(B, S, SWA_KV_HEADS, SWA_HEAD_DIM),
                                            sinks[i], rel_bias)

        y = jax.nn.sigmoid(gate_a) * (o_a @ w_proj_gdn[i]) + jax.nn.sigmoid(gate_b) * (o_b @ w_proj_swa[i])
        x = x + y @ w_out[i]

        h = rmsnorm(x, ln_ffn_g[i])
        j = i // 2
        if i % 2 == 0:
            x = x + swiglu(h, ffn_w_gate[j], ffn_w_up[j], ffn_w_down[j])
        else:
            x = x + moe_swiglu(h, router_w[j], moe_w_gate[j], moe_w_up[j], moe_w_down[j])
    return rmsnorm(x, ln_final_g)
```

```python
import functools
import math

import jax
import jax.numpy as jnp
from jax import lax
from jax.experimental import pallas as pl
from jax.experimental.pallas import tpu as pltpu

F32 = jnp.float32
BF16 = jnp.bfloat16

D_MODEL = 2048
GDN_QK_HEADS = 8
GDN_V_HEADS = 16
GDN_HEAD_DIM = 128
GDN_CONV = 4
GDN_CHUNK = 64
SWA_Q_HEADS = 32
SWA_KV_HEADS = 4
SWA_HEAD_DIM = 64
SWA_WINDOW = 128
REL_BUCKETS = 32
REL_MAX_DIST = 128
N_EXPERTS = 8
TOP_K = 2
NORM_EPS = 1e-6
GDN_KEY_DIM = GDN_QK_HEADS * GDN_HEAD_DIM
GDN_VAL_DIM = GDN_V_HEADS * GDN_HEAD_DIM
SWA_Q_DIM = SWA_Q_HEADS * SWA_HEAD_DIM
SWA_KV_DIM = SWA_KV_HEADS * SWA_HEAD_DIM

LANES = 128
VMEM_LIMIT = 56 * 1024 * 1024

COL_Q = 0
COL_K = COL_Q + GDN_KEY_DIM
COL_V = COL_K + GDN_KEY_DIM
COL_Z = COL_V + GDN_VAL_DIM
COL_SQ = COL_Z + GDN_VAL_DIM
COL_GA = COL_SQ + SWA_Q_DIM
COL_GB = COL_GA + D_MODEL
COL_SK = COL_GB + D_MODEL
COL_SV = COL_SK + SWA_KV_DIM
PROJ_COLS = COL_SV + SWA_KV_DIM
GDN_CONV_DIM = 2 * GDN_KEY_DIM + GDN_VAL_DIM

MASKED = -1e30


def _cparams(n_axes):
    return pltpu.CompilerParams(dimension_semantics=("arbitrary",) * n_axes,
                                vmem_limit_bytes=VMEM_LIMIT)


def _rms(x, g):
    return x * lax.rsqrt(jnp.mean(x * x, axis=-1, keepdims=True) + NORM_EPS) * g


def _silu(x):
    return x * jax.nn.sigmoid(x)


def _dot(a, b):
    return jnp.dot(a, b, preferred_element_type=F32)


def _dot_nt(a, b):
    return lax.dot_general(a, b, (((1,), (1,)), ((), ())), preferred_element_type=F32)


def _dot_tn(a, b):
    return lax.dot_general(a, b, (((0,), (0,)), ((), ())), preferred_element_type=F32)


def _inproj_body(x_ref, g_ref, w_ref, wba_ref, o_ref, oba_ref, h_ref):
    @pl.when(pl.program_id(1) == 0)
    def _():
        h = _rms(x_ref[...], g_ref[...]).astype(BF16)
        h_ref[...] = h
        oba_ref[...] = _dot(h, wba_ref[...])

    o_ref[...] = _dot(h_ref[...], w_ref[...]).astype(o_ref.dtype)


def in_proj(x2, g, w_main, w_ba, tm, tn):
    T, D = x2.shape
    N = w_main.shape[1]
    return pl.pallas_call(
        _inproj_body,
        grid=(T // tm, N // tn),
        in_specs=[pl.BlockSpec((tm, D), lambda i, j: (i, 0)),
                  pl.BlockSpec((1, D), lambda i, j: (0, 0)),
                  pl.BlockSpec((D, tn), lambda i, j: (0, j)),
                  pl.BlockSpec((D, LANES), lambda i, j: (0, 0))],
        out_specs=[pl.BlockSpec((tm, tn), lambda i, j: (i, j)),
                   pl.BlockSpec((tm, LANES), lambda i, j: (i, 0))],
        out_shape=[jax.ShapeDtypeStruct((T, N), BF16), jax.ShapeDtypeStruct((T, LANES), F32)],
        scratch_shapes=[pltpu.VMEM((tm, D), BF16)],
        compiler_params=_cparams(2),
        name="in_proj",
    )(x2, g, w_main, w_ba)


def _prep_body(x_ref, halo_ref, cw_ref, ba_ref, alog_ref, dtb_ref, q_ref, k_ref, v_ref, gb_ref,
               *, ts, seq, n_qk, n_v, chunk):
    first = (pl.program_id(0) * ts) % seq == 0
    for c in range(2 * n_qk + n_v):
        cs = slice(c * LANES, (c + 1) * LANES)
        cur = x_ref[:, cs].astype(F32)
        hal = halo_ref[8:16, cs].astype(F32)
        hal = jnp.where(first, 0.0, hal)
        ext = jnp.concatenate([hal, cur], axis=0)
        y = ext[5:5 + ts] * cw_ref[0:1, cs]
        y = y + ext[6:6 + ts] * cw_ref[1:2, cs]
        y = y + ext[7:7 + ts] * cw_ref[2:3, cs]
        y = y + cur * cw_ref[3:4, cs]
        y = _silu(y)
        if c < 2 * n_qk:
            y = y * lax.rsqrt(jnp.sum(y * y, axis=-1, keepdims=True) + 1e-6)
        if c < n_qk:
            q_ref[:, cs] = (y * (GDN_HEAD_DIM ** -0.5)).astype(q_ref.dtype)
        elif c < 2 * n_qk:
            k_ref[:, (c - n_qk) * LANES:(c - n_qk + 1) * LANES] = y.astype(k_ref.dtype)
        else:
            v_ref[:, (c - 2 * n_qk) * LANES:(c - 2 * n_qk + 1) * LANES] = y.astype(v_ref.dtype)

    ba = ba_ref[...]
    lane = lax.broadcasted_iota(jnp.int32, ba.shape, 1)
    pos = lax.broadcasted_iota(jnp.int32, ba.shape, 0) % chunk
    beta = jax.nn.sigmoid(ba)
    t = ba + dtb_ref[...]
    softplus = jnp.maximum(t, 0.0) + jnp.log1p(jnp.exp(-jnp.abs(t)))
    gc = -jnp.exp(alog_ref[...]) * softplus
    s = 1
    while s < chunk:
        gc = gc + jnp.where(pos >= s, pltpu.roll(gc, s, 0), 0.0)
        s *= 2
    gb_ref[...] = jnp.where(lane < n_v, beta, jnp.where(lane < 2 * n_v, gc, 0.0))


def gdn_prep(proj, ba, conv_w, alog_pad, dtb_pad, seq, ts):
    T = proj.shape[0]
    n_qk, n_v = GDN_QK_HEADS, GDN_V_HEADS
    body = functools.partial(_prep_body, ts=ts, seq=seq, n_qk=n_qk, n_v=n_v, chunk=GDN_CHUNK)
    return pl.pallas_call(
        body,
        grid=(T // ts,),
        in_specs=[pl.BlockSpec((ts, GDN_CONV_DIM), lambda i: (i, 0)),
                  pl.BlockSpec((16, GDN_CONV_DIM), lambda i: (jnp.maximum(i * (ts // 16) - 1, 0), 0)),
                  pl.BlockSpec((GDN_CONV, GDN_CONV_DIM), lambda i: (0, 0)),
                  pl.BlockSpec((ts, LANES), lambda i: (i, 0)),
                  pl.BlockSpec((1, LANES), lambda i: (0, 0)),
                  pl.BlockSpec((1, LANES), lambda i: (0, 0))],
        out_specs=[pl.BlockSpec((ts, GDN_KEY_DIM), lambda i: (i, 0)),
                   pl.BlockSpec((ts, GDN_KEY_DIM), lambda i: (i, 0)),
                   pl.BlockSpec((ts, GDN_VAL_DIM), lambda i: (i, 0)),
                   pl.BlockSpec((ts, LANES), lambda i: (i, 0))],
        out_shape=[jax.ShapeDtypeStruct((T, GDN_KEY_DIM), BF16),
                   jax.ShapeDtypeStruct((T, GDN_KEY_DIM), BF16),
                   jax.ShapeDtypeStruct((T, GDN_VAL_DIM), BF16),
                   jax.ShapeDtypeStruct((T, LANES), F32)],
        compiler_params=_cparams(1),
        name="gdn_prep",
    )(proj, proj, conv_w, ba, alog_pad, dtb_pad)


def _gdn_body(q_ref, k_ref, v_ref, z_ref, gb_ref, gcr_ref, gn_ref, o_ref,
              gbr_s, u_s, w_s, qd_s, kd_s, qk_s, gl_s, st_s, *, n_chunks, n_v):
    C = GDN_CHUNK
    hd = GDN_HEAD_DIM
    h = pl.program_id(1)
    gbr_s[...] = pltpu.roll(gb_ref[...], lax.rem(LANES - 2 * h, LANES), 1)

    row = lax.broadcasted_iota(jnp.int32, (C, C), 0)
    col = lax.broadcasted_iota(jnp.int32, (C, C), 1)
    causal = row >= col
    strict = row > col

    def phase1(c, carry):
        r0 = pl.multiple_of(c * C, C)
        rows = pl.ds(r0, C)
        q = q_ref[rows, :]
        k = k_ref[rows, :]
        qf = q.astype(F32)
        kf = k.astype(F32)
        kk = _dot_nt(k, k)
        qk = _dot_nt(q, k)
        gbc = gbr_s[rows, :]
        for j in range(2):
            bcol = gbc[:, j:j + 1]
            gcol = gbc[:, n_v + j:n_v + j + 1]
            glast = gbc[C - 1:C, n_v + j:n_v + j + 1]
            grow = gcr_ref[0, j, pl.ds(c, 1), :]
            decay = jnp.exp(jnp.where(causal, gcol - grow, -jnp.inf))
            bk = jnp.where(strict, kk * bcol * decay, 0.0) * -1.0
            nn = bk
            p = 1
            while 2 * p < C:
                bb = bk.astype(BF16)
                bk = _dot(bb, bb)
                nn = nn + bk + _dot(nn.astype(BF16), bk.astype(BF16))
                p *= 2
            nb = nn.astype(BF16)
            egc = jnp.exp(gcol)
            vb = v_ref[rows, j * hd:(j + 1) * hd].astype(F32) * bcol
            kbg = kf * (bcol * egc)
            u_s[j, rows, :] = vb + _dot(nb, vb.astype(BF16))
            w_s[j, rows, :] = (kbg + _dot(nb, kbg.astype(BF16))).astype(BF16)
            qd_s[j, rows, :] = (qf * egc).astype(BF16)
            kd_s[j, rows, :] = (kf * jnp.exp(glast - gcol)).astype(BF16)
            qk_s[j, rows, :] = (qk * decay).astype(BF16)
            gl_s[j, pl.ds(c, 1), :] = jnp.broadcast_to(jnp.exp(glast), (1, LANES))
        return carry

    lax.fori_loop(0, n_chunks, phase1, 0)

    st_s[...] = jnp.zeros_like(st_s)

    def phase2(c, carry):
        r0 = pl.multiple_of(c * C, C)
        rows = pl.ds(r0, C)
        for j in range(2):
            s = st_s[j]
            sb = s.astype(BF16)
            vnew = u_s[j, rows, :] - _dot(w_s[j, rows, :], sb)
            vnb = vnew.astype(BF16)
            o = _dot(qd_s[j, rows, :], sb) + _dot(qk_s[j, rows, :], vnb)
            st_s[j] = s * gl_s[j, pl.ds(c, 1), :] + _dot_tn(kd_s[j, rows, :], vnb)
            o = o * lax.rsqrt(jnp.mean(o * o, axis=-1, keepdims=True) + NORM_EPS) * gn_ref[...]
            o = o * _silu(z_ref[rows, j * hd:(j + 1) * hd].astype(F32))
            o_ref[rows, j * hd:(j + 1) * hd] = o.astype(o_ref.dtype)
        return carry

    lax.fori_loop(0, n_chunks, phase2, 0)


def gdn_core(qn, kn, vv, proj, gb, gcr, gnorm, batch, seq):
    T = qn.shape[0]
    n_qk, n_v, hd, C = GDN_QK_HEADS, GDN_V_HEADS, GDN_HEAD_DIM, GDN_CHUNK
    n_chunks = seq // C
    zc = COL_Z // (2 * hd)
    body = functools.partial(_gdn_body, n_chunks=n_chunks, n_v=n_v)
    return pl.pallas_call(
        body,
        grid=(batch, n_qk),
        in_specs=[pl.BlockSpec((seq, hd), lambda b, h: (b, h)),
                  pl.BlockSpec((seq, hd), lambda b, h: (b, h)),
                  pl.BlockSpec((seq, 2 * hd), lambda b, h: (b, h)),
                  pl.BlockSpec((seq, 2 * hd), lambda b, h: (b, zc + h)),
                  pl.BlockSpec((seq, LANES), lambda b, h: (b, 0)),
                  pl.BlockSpec((1, 2, n_chunks, C), lambda b, h: (b, h, 0, 0)),
                  pl.BlockSpec((1, hd), lambda b, h: (0, 0))],
        out_specs=pl.BlockSpec((seq, 2 * hd), lambda b, h: (b, h)),
        out_shape=jax.ShapeDtypeStruct((T, n_v * hd), BF16),
        scratch_shapes=[pltpu.VMEM((seq, LANES), F32),
                        pltpu.VMEM((2, seq, hd), F32),
                        pltpu.VMEM((2, seq, hd), BF16),
                        pltpu.VMEM((2, seq, hd), BF16),
                        pltpu.VMEM((2, seq, hd), BF16),
                        pltpu.VMEM((2, seq, C), BF16),
                        pltpu.VMEM((2, n_chunks, LANES), F32),
                        pltpu.VMEM((2, hd, hd), F32)],
        compiler_params=_cparams(2),
        name="gdn_core",
    )(qn, kn, vv, proj, gb, gcr, gnorm)


def _swa_body(q_ref, kp_ref, kc_ref, vp_ref, vc_ref, bias_ref, sink_ref, o_ref):
    W, dh = SWA_WINDOW, SWA_HEAD_DIM
    pairs = (SWA_Q_HEADS // SWA_KV_HEADS) // 2
    n = pl.program_id(1)
    key = lax.broadcasted_iota(jnp.int32, (1, 2 * W), 1)
    pen = jnp.where((n == 0) & (key < W), MASKED, 0.0)
    zero = jnp.zeros((2 * W, dh), BF16)
    for hk in range(SWA_KV_HEADS):
        ks = slice(hk * dh, (hk + 1) * dh)
        kb = jnp.concatenate([kp_ref[:, ks], kc_ref[:, ks]], axis=0)
        vb = jnp.concatenate([vp_ref[:, ks], vc_ref[:, ks]], axis=0)
        krhs = jnp.concatenate([jnp.concatenate([kb, zero], axis=1),
                                jnp.concatenate([zero, kb], axis=1)], axis=0)
        vrhs = jnp.concatenate([jnp.concatenate([vb, zero], axis=1),
                                jnp.concatenate([zero, vb], axis=1)], axis=0)
        qs = jnp.concatenate([q_ref[:, (hk * pairs + p) * LANES:(hk * pairs + p + 1) * LANES]
                              for p in range(pairs)], axis=0)
        qs = qs * jnp.asarray(dh ** -0.5, BF16)
        logits = _dot_nt(qs, krhs) + bias_ref[hk]
        sink = sink_ref[hk]
        halves = []
        for c in range(2):
            lg = logits[:, c * 2 * W:(c + 1) * 2 * W] + pen
            sk = sink[:, c:c + 1]
            m = jnp.maximum(jnp.max(lg, axis=-1, keepdims=True), sk)
            p = jnp.exp(lg - m)
            den = jnp.sum(p, axis=-1, keepdims=True) + jnp.exp(sk - m)
            halves.append((p / den).astype(BF16))
        out = _dot(jnp.concatenate(halves, axis=1), vrhs)
        for p in range(pairs):
            o_ref[:, (hk * pairs + p) * LANES:(hk * pairs + p + 1) * LANES] = (
                out[p * W:(p + 1) * W].astype(o_ref.dtype))


def swa_core(proj, bias_tbl, sink_tbl, batch, seq):
    T = proj.shape[0]
    W = SWA_WINDOW
    nb = seq // W
    qc, kc, vc = COL_SQ // SWA_Q_DIM, COL_SK // SWA_KV_DIM, COL_SV // SWA_KV_DIM
    cur = lambda b, n: b * nb + n
    prev = lambda b, n: jnp.maximum(b * nb + n - 1, 0)
    return pl.pallas_call(
        _swa_body,
        grid=(batch, nb),
        in_specs=[pl.BlockSpec((W, SWA_Q_DIM), lambda b, n: (cur(b, n), qc)),
                  pl.BlockSpec((W, SWA_KV_DIM), lambda b, n: (prev(b, n), kc)),
                  pl.BlockSpec((W, SWA_KV_DIM), lambda b, n: (cur(b, n), kc)),
                  pl.BlockSpec((W, SWA_KV_DIM), lambda b, n: (prev(b, n), vc)),
                  pl.BlockSpec((W, SWA_KV_DIM), lambda b, n: (cur(b, n), vc)),
                  pl.BlockSpec(bias_tbl.shape, lambda b, n: (0, 0, 0)),
                  pl.BlockSpec(sink_tbl.shape, lambda b, n: (0, 0, 0))],
        out_specs=pl.BlockSpec((W, SWA_Q_DIM), lambda b, n: (cur(b, n), 0)),
        out_shape=jax.ShapeDtypeStruct((T, SWA_Q_DIM), BF16),
        compiler_params=_cparams(2),
        name="swa_core",
    )(proj, proj, proj, proj, proj, bias_tbl, sink_tbl)


def _t5_bucket(dist):
    max_exact = REL_BUCKETS // 2
    large = max_exact + (jnp.log(jnp.maximum(dist, 1).astype(F32) / max_exact)
                         / math.log(REL_MAX_DIST / max_exact) * (REL_BUCKETS - max_exact)).astype(jnp.int32)
    large = jnp.minimum(large, REL_BUCKETS - 1)
    return jnp.where(dist < max_exact, dist, large)


def swa_tables(rel_bias, sinks):
    W, hq, hkv = SWA_WINDOW, SWA_Q_HEADS, SWA_KV_HEADS
    pairs = (hq // hkv) // 2
    qi = jnp.arange(W)[:, None]
    ki = jnp.arange(2 * W)[None, :]
    dist = qi + W - ki
    in_window = (dist >= 0) & (dist < W)
    bias = rel_bias.astype(F32)[_t5_bucket(jnp.clip(dist, 0, W - 1))]
    tbl = jnp.where(in_window[:, :, None], bias, MASKED).transpose(2, 0, 1)
    tbl = tbl.reshape(hkv, pairs, 2, W, 2 * W).transpose(0, 1, 3, 2, 4).reshape(hkv, pairs * W, 4 * W)
    sk = jnp.broadcast_to(sinks.astype(F32).reshape(hkv, pairs, 1, 2), (hkv, pairs, W, 2))
    sk = jnp.pad(sk.reshape(hkv, pairs * W, 2), ((0, 0), (0, 0), (0, LANES - 2)))
    return tbl, sk


def _merge_body(oa_ref, ob_ref, wa_ref, wb_ref, ga_ref, gb_ref, y_ref):
    a = _dot(oa_ref[...], wa_ref[...])
    b = _dot(ob_ref[...], wb_ref[...])
    y = jax.nn.sigmoid(ga_ref[...].astype(F32)) * a + jax.nn.sigmoid(gb_ref[...].astype(F32)) * b
    y_ref[...] = y.astype(y_ref.dtype)


def merge_proj(o_a, o_b, w_a, w_b, proj, tm, tn):
    T, K = o_a.shape
    N = w_a.shape[1]
    ga0, gb0 = COL_GA // tn, COL_GB // tn
    return pl.pallas_call(
        _merge_body,
        grid=(T // tm, N // tn),
        in_specs=[pl.BlockSpec((tm, K), lambda i, j: (i, 0)),
                  pl.BlockSpec((tm, K), lambda i, j: (i, 0)),
                  pl.BlockSpec((K, tn), lambda i, j: (0, j)),
                  pl.BlockSpec((K, tn), lambda i, j: (0, j)),
                  pl.BlockSpec((tm, tn), lambda i, j: (i, ga0 + j)),
                  pl.BlockSpec((tm, tn), lambda i, j: (i, gb0 + j))],
        out_specs=pl.BlockSpec((tm, tn), lambda i, j: (i, j)),
        out_shape=jax.ShapeDtypeStruct((T, N), BF16),
        compiler_params=_cparams(2),
        name="merge_proj",
    )(o_a, o_b, w_a, w_b, proj, proj)


def _resid_body(y_ref, w_ref, x_ref, o_ref):
    o_ref[...] = x_ref[...] + _dot(y_ref[...], w_ref[...])


def out_proj(y, w, x2, tm, tn):
    T, K = y.shape
    N = w.shape[1]
    return pl.pallas_call(
        _resid_body,
        grid=(T // tm, N // tn),
        in_specs=[pl.BlockSpec((tm, K), lambda i, j: (i, 0)),
                  pl.BlockSpec((K, tn), lambda i, j: (0, j)),
                  pl.BlockSpec((tm, tn), lambda i, j: (i, j))],
        out_specs=pl.BlockSpec((tm, tn), lambda i, j: (i, j)),
        out_shape=jax.ShapeDtypeStruct((T, N), F32),
        compiler_params=_cparams(2),
        name="out_proj",
    )(y, w, x2)


def _ffn_body(x_ref, g_ref, wg_ref, wu_ref, wd_ref, o_ref, h_ref):
    @pl.when(pl.program_id(1) == 0)
    def _():
        x = x_ref[...]
        h_ref[...] = _rms(x, g_ref[...]).astype(BF16)
        o_ref[...] = x

    h = h_ref[...]
    mid = (_silu(_dot(h, wg_ref[...])) * _dot(h, wu_ref[...])).astype(BF16)
    o_ref[...] += _dot(mid, wd_ref[...])


def dense_ffn(x2, g, w_gate, w_up, w_down, tm, tf):
    T, D = x2.shape
    F = w_gate.shape[1]
    return pl.pallas_call(
        _ffn_body,
        grid=(T // tm, F // tf),
        in_specs=[pl.BlockSpec((tm, D), lambda i, f: (i, 0)),
                  pl.BlockSpec((1, D), lambda i, f: (0, 0)),
                  pl.BlockSpec((D, tf), lambda i, f: (0, f)),
                  pl.BlockSpec((D, tf), lambda i, f: (0, f)),
                  pl.BlockSpec((tf, D), lambda i, f: (f, 0))],
        out_specs=pl.BlockSpec((tm, D), lambda i, f: (i, 0)),
        out_shape=jax.ShapeDtypeStruct((T, D), F32),
        scratch_shapes=[pltpu.VMEM((tm, D), BF16)],
        compiler_params=_cparams(2),
        name="dense_ffn",
    )(x2, g, w_gate, w_up, w_down)


def _router_body(x_ref, g_ref, rwt_ref, e_ref, w_ref):
    h = _rms(x_ref[...], g_ref[...]).astype(BF16)
    lt = _dot_nt(rwt_ref[...], h)
    idx = lax.broadcasted_iota(jnp.int32, lt.shape, 0)
    m1 = jnp.max(lt, axis=0, keepdims=True)
    i1 = jnp.min(jnp.where(lt == m1, idx, N_EXPERTS), axis=0, keepdims=True)
    lt2 = jnp.where(idx == i1, -jnp.inf, lt)
    m2 = jnp.max(lt2, axis=0, keepdims=True)
    i2 = jnp.min(jnp.where(lt2 == m2, idx, N_EXPERTS), axis=0, keepdims=True)
    e2 = jnp.exp(m2 - m1)
    den = 1.0 + e2
    e_ref[...] = jnp.concatenate([i1, i2], axis=0)
    w_ref[...] = jnp.concatenate([1.0 / den, e2 / den], axis=0)


def moe_router(x2, g, rw_t, tm):
    T, D = x2.shape
    return pl.pallas_call(
        _router_body,
        grid=(T // tm,),
        in_specs=[pl.BlockSpec((tm, D), lambda i: (i, 0)),
                  pl.BlockSpec((1, D), lambda i: (0, 0)),
                  pl.BlockSpec((N_EXPERTS, D), lambda i: (0, 0))],
        out_specs=[pl.BlockSpec((TOP_K, tm), lambda i: (0, i)),
                   pl.BlockSpec((TOP_K, tm), lambda i: (0, i))],
        out_shape=[jax.ShapeDtypeStruct((TOP_K, T), jnp.int32),
                   jax.ShapeDtypeStruct((TOP_K, T), F32)],
        compiler_params=_cparams(1),
        name="moe_router",
    )(x2, g, rw_t)


def _row_copy(src_hbm, idx, buf, slot, r, sem):
    return pltpu.make_async_copy(src_hbm.at[pl.ds(idx, 1), :], buf.at[slot, pl.ds(r, 1), :], sem.at[slot])


def _moe_body(be_ref, nu_ref, tok_ref, x_hbm, g_ref, wg_ref, wu_ref, wd_ref, o_ref,
              xbuf, h_ref, sem, *, tm):
    i = pl.program_id(0)
    f = pl.program_id(1)
    n_used = nu_ref[0]

    def gather(blk, slot):
        def issue(r, c):
            _row_copy(x_hbm, tok_ref[blk * tm + r], xbuf, slot, r, sem).start()
            return c
        lax.fori_loop(0, tm, issue, 0)

    @pl.when((f == 0) & (i == 0))
    def _():
        gather(0, 0)

    @pl.when((f == 0) & (i < n_used))
    def _():
        slot = i % 2

        @pl.when(i + 1 < n_used)
        def _():
            gather(i + 1, 1 - slot)

        def wait(r, c):
            _row_copy(x_hbm, 0, xbuf, slot, r, sem).wait()
            return c
        lax.fori_loop(0, tm, wait, 0)
        h_ref[...] = _rms(xbuf[slot], g_ref[...]).astype(BF16)

    @pl.when(f == 0)
    def _():
        o_ref[...] = jnp.zeros_like(o_ref)

    @pl.when(i < n_used)
    def _():
        h = h_ref[...]
        mid = (_silu(_dot(h, wg_ref[...])) * _dot(h, wu_ref[...])).astype(BF16)
        o_ref[...] += _dot(mid, wd_ref[...])


def moe_experts(x2, g, w_gate, w_up, w_down, block_e, n_used, row_tok, tm, tf):
    T, D = x2.shape
    E, _, F = w_gate.shape
    n_rows = row_tok.shape[0]
    nf = F // tf

    def fidx(i, f, nu):
        return jnp.where(i < nu[0], f, nf - 1)

    body = functools.partial(_moe_body, tm=tm)
    return pl.pallas_call(
        body,
        grid_spec=pltpu.PrefetchScalarGridSpec(
            num_scalar_prefetch=3,
            grid=(n_rows // tm, nf),
            in_specs=[pl.BlockSpec(memory_space=pl.ANY),
                      pl.BlockSpec((1, D), lambda i, f, be, nu, tok: (0, 0)),
                      pl.BlockSpec((None, D, tf), lambda i, f, be, nu, tok: (be[i], 0, fidx(i, f, nu))),
                      pl.BlockSpec((None, D, tf), lambda i, f, be, nu, tok: (be[i], 0, fidx(i, f, nu))),
                      pl.BlockSpec((None, tf, D), lambda i, f, be, nu, tok: (be[i], fidx(i, f, nu), 0))],
            out_specs=pl.BlockSpec((tm, D), lambda i, f, be, nu, tok: (i, 0)),
            scratch_shapes=[pltpu.VMEM((2, tm, D), F32),
                            pltpu.VMEM((tm, D), BF16),
                            pltpu.SemaphoreType.DMA((2,))]),
        out_shape=jax.ShapeDtypeStruct((n_rows, D), F32),
        compiler_params=_cparams(2),
        name="moe_experts",
    )(block_e, n_used, row_tok, x2, g, w_gate, w_up, w_down)


def _combine_body(dest_ref, yg_hbm, x_ref, tw_ref, g_ref, o_ref, ybuf, sem, *, tt, n_steps, final_norm):
    i = pl.program_id(0)

    def gather(blk, slot):
        def issue(r, c):
            for k in range(TOP_K):
                pltpu.make_async_copy(yg_hbm.at[pl.ds(dest_ref[(blk * tt + r) * TOP_K + k], 1), :],
                                      ybuf.at[slot, k, pl.ds(r, 1), :], sem.at[slot]).start()
            return c
        lax.fori_loop(0, tt, issue, 0)

    @pl.when(i == 0)
    def _():
        gather(0, 0)

    slot = i % 2

    @pl.when(i + 1 < n_steps)
    def _():
        gather(i + 1, 1 - slot)

    def wait(r, c):
        for k in range(TOP_K):
            pltpu.make_async_copy(yg_hbm.at[pl.ds(0, 1), :], ybuf.at[slot, k, pl.ds(r, 1), :],
                                  sem.at[slot]).wait()
        return c
    lax.fori_loop(0, tt, wait, 0)

    tw = tw_ref[...]
    y = x_ref[...] + ybuf[slot, 0] * tw[:, 0:1] + ybuf[slot, 1] * tw[:, 1:2]
    if final_norm:
        y = _rms(y, g_ref[...])
    o_ref[...] = y


def moe_combine(yg, x2, tw_col, dest, g_final, tt, final_norm):
    T, D = x2.shape
    n_steps = T // tt
    body = functools.partial(_combine_body, tt=tt, n_steps=n_steps, final_norm=final_norm)
    return pl.pallas_call(
        body,
        grid_spec=pltpu.PrefetchScalarGridSpec(
            num_scalar_prefetch=1,
            grid=(n_steps,),
            in_specs=[pl.BlockSpec(memory_space=pl.ANY),
                      pl.BlockSpec((tt, D), lambda i, d: (i, 0)),
                      pl.BlockSpec((tt, LANES), lambda i, d: (i, 0)),
                      pl.BlockSpec((1, D), lambda i, d: (0, 0))],
            out_specs=pl.BlockSpec((tt, D), lambda i, d: (i, 0)),
            scratch_shapes=[pltpu.VMEM((2, TOP_K, tt, D), F32),
                            pltpu.SemaphoreType.DMA((2,))]),
        out_shape=jax.ShapeDtypeStruct((T, D), F32),
        compiler_params=_cparams(1),
        name="moe_combine",
    )(dest, yg, x2, tw_col, g_final)


def moe_plan(top_e, tm):
    T = top_e.shape[1]
    TK = T * TOP_K
    flat_e = top_e.T.reshape(TK)
    onehot = (flat_e[:, None] == jnp.arange(N_EXPERTS, dtype=jnp.int32)[None, :]).astype(jnp.int32)
    cum = jnp.cumsum(onehot, axis=0)
    counts = cum[-1]
    rank = jnp.sum(onehot * (cum - 1), axis=1)
    padded = (counts + tm - 1) // tm * tm
    pad_end = jnp.cumsum(padded)
    pad_start = pad_end - padded
    dest = jnp.sum(onehot * pad_start[None, :], axis=1) + rank
    n_rows = TK + N_EXPERTS * tm
    n_blocks = n_rows // tm
    row_tok = jnp.zeros((n_rows,), jnp.int32).at[dest].set(jnp.arange(TK, dtype=jnp.int32) // TOP_K)
    block_e = jnp.minimum(jnp.searchsorted(pad_end, jnp.arange(n_blocks, dtype=jnp.int32) * tm, side='right'),
                          N_EXPERTS - 1).astype(jnp.int32)
    n_used = (pad_end[-1:] // tm).astype(jnp.int32)
    return dest.astype(jnp.int32), row_tok, block_e, n_used


def moe_block(x2, g, router_w, w_gate, w_up, w_down, g_final, final_norm, tm_r, tm_e, tf, tt):
    top_e, top_w = moe_router(x2, g, router_w.T.astype(BF16), tm_r)
    dest, row_tok, block_e, n_used = moe_plan(top_e, tm_e)
    yg = moe_experts(x2, g, w_gate, w_up, w_down, block_e, n_used, row_tok, tm_e, tf)
    tw_col = jnp.pad(top_w.T, ((0, 0), (0, LANES - TOP_K)))
    return moe_combine(yg, x2, tw_col, dest, g_final, tt, final_norm)


def _norm_body(x_ref, g_ref, o_ref):
    o_ref[...] = _rms(x_ref[...], g_ref[...])


def final_norm(x2, g, tm):
    T, D = x2.shape
    return pl.pallas_call(
        _norm_body,
        grid=(T // tm,),
        in_specs=[pl.BlockSpec((tm, D), lambda i: (i, 0)), pl.BlockSpec((1, D), lambda i: (0, 0))],
        out_specs=pl.BlockSpec((tm, D), lambda i: (i, 0)),
        out_shape=jax.ShapeDtypeStruct((T, D), F32),
        compiler_params=_cparams(1),
        name="final_norm",
    )(x2, g)


def _split_w_in(w):
    sizes = (GDN_KEY_DIM, GDN_KEY_DIM, GDN_VAL_DIM, GDN_VAL_DIM, GDN_V_HEADS, GDN_V_HEADS,
             SWA_Q_DIM, SWA_KV_DIM, SWA_KV_DIM, D_MODEL, D_MODEL)
    offs = [0]
    for s in sizes:
        offs.append(offs[-1] + s)
    q, k, v, z, b, a, sq, sk, sv, ga, gb = [w[:, offs[n]:offs[n + 1]] for n in range(len(sizes))]
    w_main = jnp.concatenate([q, k, v, z, sq, ga, gb, sk, sv], axis=1).astype(BF16)
    w_ba = jnp.pad(jnp.concatenate([b, a], axis=1), ((0, 0), (0, LANES - 2 * GDN_V_HEADS))).astype(BF16)
    return w_main, w_ba


def _pad_lanes(vec, offset):
    return jnp.pad(vec.astype(F32), (offset, LANES - offset - vec.shape[0])).reshape(1, LANES)


def mixer_layer(x2, batch, seq, ln_g, w_in, conv_w, a_log, dt_bias, gdn_norm_g, bias_tbl, sink_tbl,
                w_proj_gdn, w_proj_swa, w_out):
    T = x2.shape[0]
    w_main, w_ba = _split_w_in(w_in)
    proj, ba = in_proj(x2, ln_g.reshape(1, -1), w_main, w_ba, tm=1024, tn=1280)
    qn, kn, vv, gb = gdn_prep(proj, ba, conv_w, _pad_lanes(a_log, GDN_V_HEADS),
                              _pad_lanes(dt_bias, GDN_V_HEADS), seq, ts=256)
    n_chunks = seq // GDN_CHUNK
    gcr = gb[:, GDN_V_HEADS:2 * GDN_V_HEADS].reshape(batch, n_chunks, GDN_CHUNK, GDN_V_HEADS)
    gcr = gcr.transpose(0, 3, 1, 2)
    o_a = gdn_core(qn, kn, vv, proj, gb, gcr, gdn_norm_g.reshape(1, -1).astype(F32), batch, seq)
    o_b = swa_core(proj, bias_tbl, sink_tbl, batch, seq)
    y = merge_proj(o_a, o_b, w_proj_gdn.astype(BF16), w_proj_swa.astype(BF16), proj, tm=1024, tn=512)
    return out_proj(y, w_out.astype(BF16), x2, tm=1024, tn=512)


def kernel(x, ln_mix_g, w_in, conv_w, a_log, dt_bias, gdn_norm_g, sinks, rel_bias, w_proj_gdn, w_proj_swa,
           w_out, ln_ffn_g, ffn_w_gate, ffn_w_up, ffn_w_down, router_w, moe_w_gate, moe_w_up, moe_w_down,
           ln_final_g):
    B, S, D = x.shape
    depth = w_in.shape[0]
    x2 = x.reshape(B * S, D)
    g_final = ln_final_g.reshape(1, D)
    normed = False
    for i in range(depth):
        bias_tbl, sink_tbl = swa_tables(rel_bias, sinks[i])
        x2 = mixer_layer(x2, B, S, ln_mix_g[i], w_in[i], conv_w[i], a_log[i], dt_bias[i], gdn_norm_g[i],
                         bias_tbl, sink_tbl, w_proj_gdn[i], w_proj_swa[i], w_out[i])
        g_ffn = ln_ffn_g[i].reshape(1, D)
        j = i // 2
        if i % 2 == 0:
            f = ffn_w_gate.shape[2]
            fpad = (-f) % 512
            wg = jnp.pad(ffn_w_gate[j], ((0, 0), (0, fpad))).astype(BF16)
            wu = jnp.pad(ffn_w_up[j], ((0, 0), (0, fpad))).astype(BF16)
            wd = jnp.pad(ffn_w_down[j], ((0, fpad), (0, 0))).astype(BF16)
            x2 = dense_ffn(x2, g_ffn, wg, wu, wd, tm=512, tf=512)
        else:
            last = i == depth - 1
            x2 = moe_block(x2, g_ffn, router_w[j], moe_w_gate[j].astype(BF16), moe_w_up[j].astype(BF16),
                           moe_w_down[j].astype(BF16), g_final, last, tm_r=512, tm_e=512, tf=512, tt=256)
            normed = last
    if not normed:
        x2 = final_norm(x2, g_final, tm=512)
    return x2.reshape(B, S, D)
```

```python
import functools
import math

import jax
import jax.numpy as jnp
from jax import lax
from jax.experimental import pallas as pl
from jax.experimental.pallas import tpu as pltpu

F32 = jnp.float32
BF16 = jnp.bfloat16

D_MODEL = 2048
GDN_QK_HEADS = 8
GDN_V_HEADS = 16
GDN_HEAD_DIM = 128
GDN_CONV = 4
GDN_CHUNK = 64
SWA_Q_HEADS = 32
SWA_KV_HEADS = 4
SWA_HEAD_DIM = 64
SWA_WINDOW = 128
REL_BUCKETS = 32
REL_MAX_DIST = 128
N_EXPERTS = 8
TOP_K = 2
NORM_EPS = 1e-6
GDN_KEY_DIM = GDN_QK_HEADS * GDN_HEAD_DIM
GDN_VAL_DIM = GDN_V_HEADS * GDN_HEAD_DIM
SWA_Q_DIM = SWA_Q_HEADS * SWA_HEAD_DIM
SWA_KV_DIM = SWA_KV_HEADS * SWA_HEAD_DIM

LANES = 128
VMEM_LIMIT = 56 * 1024 * 1024

COL_Q = 0
COL_K = COL_Q + GDN_KEY_DIM
COL_V = COL_K + GDN_KEY_DIM
COL_Z = COL_V + GDN_VAL_DIM
COL_SQ = COL_Z + GDN_VAL_DIM
COL_GA = COL_SQ + SWA_Q_DIM
COL_GB = COL_GA + D_MODEL
COL_SK = COL_GB + D_MODEL
COL_SV = COL_SK + SWA_KV_DIM
PROJ_COLS = COL_SV + SWA_KV_DIM
GDN_CONV_DIM = 2 * GDN_KEY_DIM + GDN_VAL_DIM

MASKED = -1e30


def _cparams(n_axes):
    return pltpu.CompilerParams(dimension_semantics=("arbitrary",) * n_axes,
                                vmem_limit_bytes=VMEM_LIMIT)


def _rms(x, g):
    return x * lax.rsqrt(jnp.mean(x * x, axis=-1, keepdims=True) + NORM_EPS) * g


def _silu(x):
    return x * jax.nn.sigmoid(x)


def _dot(a, b):
    return jnp.dot(a, b, preferred_element_type=F32)


def _dot_nt(a, b):
    return lax.dot_general(a, b, (((1,), (1,)), ((), ())), preferred_element_type=F32)


def _dot_tn(a, b):
    return lax.dot_general(a, b, (((0,), (0,)), ((), ())), preferred_element_type=F32)


def _inproj_body(x_ref, g_ref, w_ref, wba_ref, o_ref, oba_ref, h_ref):
    @pl.when(pl.program_id(1) == 0)
    def _():
        h = _rms(x_ref[...], g_ref[...]).astype(BF16)
        h_ref[...] = h
        oba_ref[...] = _dot(h, wba_ref[...])

    o_ref[...] = _dot(h_ref[...], w_ref[...]).astype(o_ref.dtype)


def in_proj(x2, g, w_main, w_ba, tm, tn):
    T, D = x2.shape
    N = w_main.shape[1]
    return pl.pallas_call(
        _inproj_body,
        grid=(T // tm, N // tn),
        in_specs=[pl.BlockSpec((tm, D), lambda i, j: (i, 0)),
                  pl.BlockSpec((1, D), lambda i, j: (0, 0)),
                  pl.BlockSpec((D, tn), lambda i, j: (0, j)),
                  pl.BlockSpec((D, LANES), lambda i, j: (0, 0))],
        out_specs=[pl.BlockSpec((tm, tn), lambda i, j: (i, j)),
                   pl.BlockSpec((tm, LANES), lambda i, j: (i, 0))],
        out_shape=[jax.ShapeDtypeStruct((T, N), BF16), jax.ShapeDtypeStruct((T, LANES), F32)],
        scratch_shapes=[pltpu.VMEM((tm, D), BF16)],
        compiler_params=_cparams(2),
        name="in_proj",
    )(x2, g, w_main, w_ba)


def _prep_body(x_ref, halo_ref, cw_ref, ba_ref, alog_ref, dtb_ref, q_ref, k_ref, v_ref, gb_ref,
               *, ts, seq, n_qk, n_v, chunk):
    first = (pl.program_id(0) * ts) % seq == 0
    for c in range(2 * n_qk + n_v):
        cs = slice(c * LANES, (c + 1) * LANES)
        cur = x_ref[:, cs].astype(F32)
        hal = halo_ref[8:16, cs].astype(F32)
        hal = jnp.where(first, 0.0, hal)
        ext = jnp.concatenate([hal, cur], axis=0)
        y = ext[5:5 + ts] * cw_ref[0:1, cs]
        y = y + ext[6:6 + ts] * cw_ref[1:2, cs]
        y = y + ext[7:7 + ts] * cw_ref[2:3, cs]
        y = y + cur * cw_ref[3:4, cs]
        y = _silu(y)
        if c < 2 * n_qk:
            y = y * lax.rsqrt(jnp.sum(y * y, axis=-1, keepdims=True) + 1e-6)
        if c < n_qk:
            q_ref[:, cs] = (y * (GDN_HEAD_DIM ** -0.5)).astype(q_ref.dtype)
        elif c < 2 * n_qk:
            k_ref[:, (c - n_qk) * LANES:(c - n_qk + 1) * LANES] = y.astype(k_ref.dtype)
        else:
            v_ref[:, (c - 2 * n_qk) * LANES:(c - 2 * n_qk + 1) * LANES] = y.astype(v_ref.dtype)

    ba = ba_ref[...]
    lane = lax.broadcasted_iota(jnp.int32, ba.shape, 1)
    pos = lax.broadcasted_iota(jnp.int32, ba.shape, 0) % chunk
    beta = jax.nn.sigmoid(ba)
    t = ba + dtb_ref[...]
    softplus = jnp.maximum(t, 0.0) + jnp.log1p(jnp.exp(-jnp.abs(t)))
    gc = -jnp.exp(alog_ref[...]) * softplus
    s = 1
    while s < chunk:
        gc = gc + jnp.where(pos >= s, pltpu.roll(gc, s, 0), 0.0)
        s *= 2
    gb_ref[...] = jnp.where(lane < n_v, beta, jnp.where(lane < 2 * n_v, gc, 0.0))


def gdn_prep(proj, ba, conv_w, alog_pad, dtb_pad, seq, ts):
    T = proj.shape[0]
    n_qk, n_v = GDN_QK_HEADS, GDN_V_HEADS
    body = functools.partial(_prep_body, ts=ts, seq=seq, n_qk=n_qk, n_v=n_v, chunk=GDN_CHUNK)
    return pl.pallas_call(
        body,
        grid=(T // ts,),
        in_specs=[pl.BlockSpec((ts, GDN_CONV_DIM), lambda i: (i, 0)),
                  pl.BlockSpec((16, GDN_CONV_DIM), lambda i: (jnp.maximum(i * (ts // 16) - 1, 0), 0)),
                  pl.BlockSpec((GDN_CONV, GDN_CONV_DIM), lambda i: (0, 0)),
                  pl.BlockSpec((ts, LANES), lambda i: (i, 0)),
                  pl.BlockSpec((1, LANES), lambda i: (0, 0)),
                  pl.BlockSpec((1, LANES), lambda i: (0, 0))],
        out_specs=[pl.BlockSpec((ts, GDN_KEY_DIM), lambda i: (i, 0)),
                   pl.BlockSpec((ts, GDN_KEY_DIM), lambda i: (i, 0)),
                   pl.BlockSpec((ts, GDN_VAL_DIM), lambda i: (i, 0)),
                   pl.BlockSpec((ts, LANES), lambda i: (i, 0))],
        out_shape=[jax.ShapeDtypeStruct((T, GDN_KEY_DIM), BF16),
                   jax.ShapeDtypeStruct((T, GDN_KEY_DIM), BF16),
                   jax.ShapeDtypeStruct((T, GDN_VAL_DIM), BF16),
                   jax.ShapeDtypeStruct((T, LANES), F32)],
        compiler_params=_cparams(1),
        name="gdn_prep",
    )(proj, proj, conv_w, ba, alog_pad, dtb_pad)


def _gdn_body(q_ref, k_ref, v_ref, z_ref, gb_ref, gcr_ref, gn_ref, o_ref,
              u_s, wq_s, qkd_s, gl_s, st_s, *, n_chunks, n_qk, n_v):
    C = GDN_CHUNK
    hd = GDN_HEAD_DIM
    rep = n_v // n_qk

    @pl.when(pl.program_id(1) == 0)
    def _():
        st_s[...] = jnp.zeros_like(st_s)

    row = lax.broadcasted_iota(jnp.int32, (C, C), 0)
    col = lax.broadcasted_iota(jnp.int32, (C, C), 1)
    causal = row >= col
    strict = row > col

    hs = range(n_qk)
    hvs = range(n_v)

    def phase1(c, carry):
        rows = pl.ds(pl.multiple_of(c * C, C), C)
        rows2 = pl.ds(pl.multiple_of(c * 2 * C, 2 * C), C)
        rows2b = pl.ds(pl.multiple_of(c * 2 * C, 2 * C) + C, C)
        gbc = gb_ref[rows, :]
        q = [q_ref[rows, h * hd:(h + 1) * hd] for h in hs]
        k = [k_ref[rows, h * hd:(h + 1) * hd] for h in hs]
        kk = [_dot_nt(k[h], k[h]) for h in hs]
        qk = [_dot_nt(q[h], k[h]) for h in hs]
        bcol = [gbc[:, hv:hv + 1] for hv in hvs]
        gcol = [gbc[:, n_v + hv:n_v + hv + 1] for hv in hvs]
        glast = [gbc[C - 1:C, n_v + hv:n_v + hv + 1] for hv in hvs]
        decay = [jnp.exp(jnp.where(causal, gcol[hv] - gcr_ref[0, hv, pl.ds(c, 1), :], -jnp.inf)) for hv in hvs]
        bk = [jnp.where(strict, kk[hv // rep] * bcol[hv] * decay[hv], 0.0) * -1.0 for hv in hvs]
        nn = list(bk)
        egc = None
        p = 1
        while 2 * p < C:
            bb = [x.astype(BF16) for x in bk]
            bk = [_dot(x, x) for x in bb]
            if egc is None:
                egc = [jnp.exp(g) for g in gcol]
                for hv in hvs:
                    qf = q[hv // rep].astype(F32)
                    kf = k[hv // rep].astype(F32)
                    wq_s[hv, rows2b, :] = (qf * egc[hv]).astype(BF16)
                    kd = kf * jnp.exp(glast[hv] - gcol[hv])
                    qkd_s[hv, c, 0:C, :] = (qk[hv // rep] * decay[hv]).astype(BF16)
                    qkd_s[hv, c, C:3 * C, :] = kd.T.astype(BF16)
                    gl_s[hv, pl.ds(c, 1), :] = jnp.broadcast_to(jnp.exp(glast[hv]), (1, LANES))
            nn = [n + b + _dot(n.astype(BF16), b.astype(BF16)) for n, b in zip(nn, bk)]
            p *= 2
        nb = [n.astype(BF16) for n in nn]
        vb = [v_ref[rows, hv * hd:(hv + 1) * hd].astype(F32) * bcol[hv] for hv in hvs]
        kbg = [k[hv // rep].astype(F32) * (bcol[hv] * egc[hv]) for hv in hvs]
        uu = [_dot(nb[hv], vb[hv].astype(BF16)) for hv in hvs]
        ww = [_dot(nb[hv], kbg[hv].astype(BF16)) for hv in hvs]
        for hv in hvs:
            u_s[hv, rows, :] = vb[hv] + uu[hv]
            wq_s[hv, rows2, :] = (kbg[hv] + ww[hv]).astype(BF16)
        return carry

    lax.fori_loop(0, n_chunks, phase1, 0)

    def phase2(c, carry):
        rows = pl.ds(pl.multiple_of(c * C, C), C)
        rows_wq = pl.ds(pl.multiple_of(c * 2 * C, 2 * C), 2 * C)
        sb = [st_s[hv].astype(BF16) for hv in hvs]
        ws = [_dot(wq_s[hv, rows_wq, :], sb[hv]) for hv in hvs]
        vnb = [(u_s[hv, rows, :] - ws[hv][0:C]).astype(BF16) for hv in hvs]
        ov = [_dot(qkd_s[hv, c], vnb[hv]) for hv in hvs]
        for hv in hvs:
            st_s[hv] = st_s[hv] * gl_s[hv, pl.ds(c, 1), :] + ov[hv][C:3 * C]
        for hv in hvs:
            o = ws[hv][C:2 * C] + ov[hv][0:C]
            o = o * lax.rsqrt(jnp.mean(o * o, axis=-1, keepdims=True) + NORM_EPS) * gn_ref[...]
            o = o * _silu(z_ref[rows, hv * hd:(hv + 1) * hd].astype(F32))
            o_ref[rows, hv * hd:(hv + 1) * hd] = o.astype(o_ref.dtype)
        return carry

    lax.fori_loop(0, n_chunks, phase2, 0)


def gdn_core(qn, kn, vv, proj, gb, gcr, gnorm, batch, seq, sb):
    T = qn.shape[0]
    n_qk, n_v, hd, C = GDN_QK_HEADS, GDN_V_HEADS, GDN_HEAD_DIM, GDN_CHUNK
    nc = sb // C
    nsb = seq // sb
    zc = COL_Z // (n_v * hd)
    body = functools.partial(_gdn_body, n_chunks=nc, n_qk=n_qk, n_v=n_v)
    blk = lambda b, s: (b * nsb + s, 0)
    return pl.pallas_call(
        body,
        grid=(batch, nsb),
        in_specs=[pl.BlockSpec((sb, n_qk * hd), blk),
                  pl.BlockSpec((sb, n_qk * hd), blk),
                  pl.BlockSpec((sb, n_v * hd), blk),
                  pl.BlockSpec((sb, n_v * hd), lambda b, s: (b * nsb + s, zc)),
                  pl.BlockSpec((sb, LANES), blk),
                  pl.BlockSpec((1, n_v, nc, C), lambda b, s: (b, 0, s, 0)),
                  pl.BlockSpec((1, hd), lambda b, s: (0, 0))],
        out_specs=pl.BlockSpec((sb, n_v * hd), blk),
        out_shape=jax.ShapeDtypeStruct((T, n_v * hd), BF16),
        scratch_shapes=[pltpu.VMEM((n_v, sb, hd), F32),
                        pltpu.VMEM((n_v, 2 * sb, hd), BF16),
                        pltpu.VMEM((n_v, nc, 3 * C, C), BF16),
                        pltpu.VMEM((n_v, nc, LANES), F32),
                        pltpu.VMEM((n_v, hd, hd), F32)],
        compiler_params=_cparams(2),
        name="gdn_core",
    )(qn, kn, vv, proj, gb, gcr, gnorm)


def _swa_body(q_ref, kp_ref, kc_ref, vp_ref, vc_ref, bias_ref, sink_ref, o_ref):
    W, dh = SWA_WINDOW, SWA_HEAD_DIM
    pairs = (SWA_Q_HEADS // SWA_KV_HEADS) // 2
    n = pl.program_id(1)
    key = lax.broadcasted_iota(jnp.int32, (1, 2 * W), 1)
    pen = jnp.where((n == 0) & (key < W), MASKED, 0.0)
    zero = jnp.zeros((2 * W, dh), BF16)
    for hk in range(SWA_KV_HEADS):
        ks = slice(hk * dh, (hk + 1) * dh)
        kb = jnp.concatenate([kp_ref[:, ks], kc_ref[:, ks]], axis=0)
        vb = jnp.concatenate([vp_ref[:, ks], vc_ref[:, ks]], axis=0)
        krhs = jnp.concatenate([jnp.concatenate([kb, zero], axis=1),
                                jnp.concatenate([zero, kb], axis=1)], axis=0)
        vrhs = jnp.concatenate([jnp.concatenate([vb, zero], axis=1),
                                jnp.concatenate([zero, vb], axis=1)], axis=0)
        qs = jnp.concatenate([q_ref[:, (hk * pairs + p) * LANES:(hk * pairs + p + 1) * LANES]
                              for p in range(pairs)], axis=0)
        qs = qs * jnp.asarray(dh ** -0.5, BF16)
        logits = _dot_nt(qs, krhs) + bias_ref[hk]
        sink = sink_ref[hk]
        halves = []
        for c in range(2):
            lg = logits[:, c * 2 * W:(c + 1) * 2 * W] + pen
            sk = sink[:, c:c + 1]
            m = jnp.maximum(jnp.max(lg, axis=-1, keepdims=True), sk)
            p = jnp.exp(lg - m)
            den = jnp.sum(p, axis=-1, keepdims=True) + jnp.exp(sk - m)
            halves.append((p / den).astype(BF16))
        out = _dot(jnp.concatenate(halves, axis=1), vrhs)
        for p in range(pairs):
            o_ref[:, (hk * pairs + p) * LANES:(hk * pairs + p + 1) * LANES] = (
                out[p * W:(p + 1) * W].astype(o_ref.dtype))


def swa_core(proj, bias_tbl, sink_tbl, batch, seq):
    T = proj.shape[0]
    W = SWA_WINDOW
    nb = seq // W
    qc, kc, vc = COL_SQ // SWA_Q_DIM, COL_SK // SWA_KV_DIM, COL_SV // SWA_KV_DIM
    cur = lambda b, n: b * nb + n
    prev = lambda b, n: jnp.maximum(b * nb + n - 1, 0)
    return pl.pallas_call(
        _swa_body,
        grid=(batch, nb),
        in_specs=[pl.BlockSpec((W, SWA_Q_DIM), lambda b, n: (cur(b, n), qc)),
                  pl.BlockSpec((W, SWA_KV_DIM), lambda b, n: (prev(b, n), kc)),
                  pl.BlockSpec((W, SWA_KV_DIM), lambda b, n: (cur(b, n), kc)),
                  pl.BlockSpec((W, SWA_KV_DIM), lambda b, n: (prev(b, n), vc)),
                  pl.BlockSpec((W, SWA_KV_DIM), lambda b, n: (cur(b, n), vc)),
                  pl.BlockSpec(bias_tbl.shape, lambda b, n: (0, 0, 0)),
                  pl.BlockSpec(sink_tbl.shape, lambda b, n: (0, 0, 0))],
        out_specs=pl.BlockSpec((W, SWA_Q_DIM), lambda b, n: (cur(b, n), 0)),
        out_shape=jax.ShapeDtypeStruct((T, SWA_Q_DIM), BF16),
        compiler_params=_cparams(2),
        name="swa_core",
    )(proj, proj, proj, proj, proj, bias_tbl, sink_tbl)


def _t5_bucket(dist):
    max_exact = REL_BUCKETS // 2
    large = max_exact + (jnp.log(jnp.maximum(dist, 1).astype(F32) / max_exact)
                         / math.log(REL_MAX_DIST / max_exact) * (REL_BUCKETS - max_exact)).astype(jnp.int32)
    large = jnp.minimum(large, REL_BUCKETS - 1)
    return jnp.where(dist < max_exact, dist, large)


def swa_tables(rel_bias, sinks):
    W, hq, hkv = SWA_WINDOW, SWA_Q_HEADS, SWA_KV_HEADS
    pairs = (hq // hkv) // 2
    qi = jnp.arange(W)[:, None]
    ki = jnp.arange(2 * W)[None, :]
    dist = qi + W - ki
    in_window = (dist >= 0) & (dist < W)
    bias = rel_bias.astype(F32)[_t5_bucket(jnp.clip(dist, 0, W - 1))]
    tbl = jnp.where(in_window[:, :, None], bias, MASKED).transpose(2, 0, 1)
    tbl = tbl.reshape(hkv, pairs, 2, W, 2 * W).transpose(0, 1, 3, 2, 4).reshape(hkv, pairs * W, 4 * W)
    sk = jnp.broadcast_to(sinks.astype(F32).reshape(hkv, pairs, 1, 2), (hkv, pairs, W, 2))
    sk = jnp.pad(sk.reshape(hkv, pairs * W, 2), ((0, 0), (0, 0), (0, LANES - 2)))
    return tbl, sk


def _merge_body(oa_ref, ob_ref, wa_ref, wb_ref, ga_ref, gb_ref, y_ref):
    a = _dot(oa_ref[...], wa_ref[...])
    b = _dot(ob_ref[...], wb_ref[...])
    y = jax.nn.sigmoid(ga_ref[...].astype(F32)) * a + jax.nn.sigmoid(gb_ref[...].astype(F32)) * b
    y_ref[...] = y.astype(y_ref.dtype)


def merge_proj(o_a, o_b, w_a, w_b, proj, tm, tn):
    T, K = o_a.shape
    N = w_a.shape[1]
    ga0, gb0 = COL_GA // tn, COL_GB // tn
    return pl.pallas_call(
        _merge_body,
        grid=(T // tm, N // tn),
        in_specs=[pl.BlockSpec((tm, K), lambda i, j: (i, 0)),
                  pl.BlockSpec((tm, K), lambda i, j: (i, 0)),
                  pl.BlockSpec((K, tn), lambda i, j: (0, j)),
                  pl.BlockSpec((K, tn), lambda i, j: (0, j)),
                  pl.BlockSpec((tm, tn), lambda i, j: (i, ga0 + j)),
                  pl.BlockSpec((tm, tn), lambda i, j: (i, gb0 + j))],
        out_specs=pl.BlockSpec((tm, tn), lambda i, j: (i, j)),
        out_shape=jax.ShapeDtypeStruct((T, N), BF16),
        compiler_params=_cparams(2),
        name="merge_proj",
    )(o_a, o_b, w_a, w_b, proj, proj)


def _resid_body(y_ref, w_ref, x_ref, o_ref):
    o_ref[...] = x_ref[...] + _dot(y_ref[...], w_ref[...])


def out_proj(y, w, x2, tm, tn):
    T, K = y.shape
    N = w.shape[1]
    return pl.pallas_call(
        _resid_body,
        grid=(T // tm, N // tn),
        in_specs=[pl.BlockSpec((tm, K), lambda i, j: (i, 0)),
                  pl.BlockSpec((K, tn), lambda i, j: (0, j)),
                  pl.BlockSpec((tm, tn), lambda i, j: (i, j))],
        out_specs=pl.BlockSpec((tm, tn), lambda i, j: (i, j)),
        out_shape=jax.ShapeDtypeStruct((T, N), F32),
        compiler_params=_cparams(2),
        name="out_proj",
    )(y, w, x2)


def _ffn_body(x_ref, g_ref, wg_ref, wu_ref, wd_ref, o_ref, h_ref):
    @pl.when(pl.program_id(1) == 0)
    def _():
        x = x_ref[...]
        h_ref[...] = _rms(x, g_ref[...]).astype(BF16)
        o_ref[...] = x

    h = h_ref[...]
    mid = (_silu(_dot(h, wg_ref[...])) * _dot(h, wu_ref[...])).astype(BF16)
    o_ref[...] += _dot(mid, wd_ref[...])


def dense_ffn(x2, g, w_gate, w_up, w_down, tm, tf):
    T, D = x2.shape
    F = w_gate.shape[1]
    return pl.pallas_call(
        _ffn_body,
        grid=(T // tm, F // tf),
        in_specs=[pl.BlockSpec((tm, D), lambda i, f: (i, 0)),
                  pl.BlockSpec((1, D), lambda i, f: (0, 0)),
                  pl.BlockSpec((D, tf), lambda i, f: (0, f)),
                  pl.BlockSpec((D, tf), lambda i, f: (0, f)),
                  pl.BlockSpec((tf, D), lambda i, f: (f, 0))],
        out_specs=pl.BlockSpec((tm, D), lambda i, f: (i, 0)),
        out_shape=jax.ShapeDtypeStruct((T, D), F32),
        scratch_shapes=[pltpu.VMEM((tm, D), BF16)],
        compiler_params=_cparams(2),
        name="dense_ffn",
    )(x2, g, w_gate, w_up, w_down)


def _router_body(x_ref, g_ref, rwt_ref, e_ref, w_ref):
    h = _rms(x_ref[...], g_ref[...]).astype(BF16)
    lt = _dot_nt(rwt_ref[...], h)
    idx = lax.broadcasted_iota(jnp.int32, lt.shape, 0)
    m1 = jnp.max(lt, axis=0, keepdims=True)
    i1 = jnp.min(jnp.where(lt == m1, idx, N_EXPERTS), axis=0, keepdims=True)
    lt2 = jnp.where(idx == i1, -jnp.inf, lt)
    m2 = jnp.max(lt2, axis=0, keepdims=True)
    i2 = jnp.min(jnp.where(lt2 == m2, idx, N_EXPERTS), axis=0, keepdims=True)
    e2 = jnp.exp(m2 - m1)
    den = 1.0 + e2
    e_ref[...] = jnp.concatenate([i1, i2], axis=0)
    w_ref[...] = jnp.concatenate([1.0 / den, e2 / den], axis=0)


def moe_router(x2, g, rw_t, tm):
    T, D = x2.shape
    return pl.pallas_call(
        _router_body,
        grid=(T // tm,),
        in_specs=[pl.BlockSpec((tm, D), lambda i: (i, 0)),
                  pl.BlockSpec((1, D), lambda i: (0, 0)),
                  pl.BlockSpec((N_EXPERTS, D), lambda i: (0, 0))],
        out_specs=[pl.BlockSpec((TOP_K, tm), lambda i: (0, i)),
                   pl.BlockSpec((TOP_K, tm), lambda i: (0, i))],
        out_shape=[jax.ShapeDtypeStruct((TOP_K, T), jnp.int32),
                   jax.ShapeDtypeStruct((TOP_K, T), F32)],
        compiler_params=_cparams(1),
        name="moe_router",
    )(x2, g, rw_t)


def _row_copy(src_hbm, idx, buf, slot, r, sem):
    return pltpu.make_async_copy(src_hbm.at[pl.ds(idx, 1), :], buf.at[slot, pl.ds(r, 1), :], sem.at[slot])


def _moe_body(be_ref, nu_ref, tok_ref, x_hbm, g_ref, wg_ref, wu_ref, wd_ref, o_ref,
              xbuf, h_ref, sem, *, tm):
    i = pl.program_id(0)
    f = pl.program_id(1)
    n_used = nu_ref[0]

    def gather(blk, slot):
        def issue(r, c):
            _row_copy(x_hbm, tok_ref[blk * tm + r], xbuf, slot, r, sem).start()
            return c
        lax.fori_loop(0, tm, issue, 0)

    @pl.when((f == 0) & (i == 0))
    def _():
        gather(0, 0)

    @pl.when((f == 0) & (i < n_used))
    def _():
        slot = i % 2

        @pl.when(i + 1 < n_used)
        def _():
            gather(i + 1, 1 - slot)

        def wait(r, c):
            _row_copy(x_hbm, 0, xbuf, slot, r, sem).wait()
            return c
        lax.fori_loop(0, tm, wait, 0)
        h_ref[...] = _rms(xbuf[slot], g_ref[...]).astype(BF16)

    @pl.when(f == 0)
    def _():
        o_ref[...] = jnp.zeros_like(o_ref)

    @pl.when(i < n_used)
    def _():
        h = h_ref[...]
        mid = (_silu(_dot(h, wg_ref[...])) * _dot(h, wu_ref[...])).astype(BF16)
        o_ref[...] += _dot(mid, wd_ref[...])


def moe_experts(x2, g, w_gate, w_up, w_down, block_e, n_used, row_tok, tm, tf):
    T, D = x2.shape
    E, _, F = w_gate.shape
    n_rows = row_tok.shape[0]
    nf = F // tf

    def fidx(i, f, nu):
        return jnp.where(i < nu[0], f, nf - 1)

    body = functools.partial(_moe_body, tm=tm)
    return pl.pallas_call(
        body,
        grid_spec=pltpu.PrefetchScalarGridSpec(
            num_scalar_prefetch=3,
            grid=(n_rows // tm, nf),
            in_specs=[pl.BlockSpec(memory_space=pl.ANY),
                      pl.BlockSpec((1, D), lambda i, f, be, nu, tok: (0, 0)),
                      pl.BlockSpec((None, D, tf), lambda i, f, be, nu, tok: (be[i], 0, fidx(i, f, nu))),
                      pl.BlockSpec((None, D, tf), lambda i, f, be, nu, tok: (be[i], 0, fidx(i, f, nu))),
                      pl.BlockSpec((None, tf, D), lambda i, f, be, nu, tok: (be[i], fidx(i, f, nu), 0))],
            out_specs=pl.BlockSpec((tm, D), lambda i, f, be, nu, tok: (i, 0)),
            scratch_shapes=[pltpu.VMEM((2, tm, D), F32),
                            pltpu.VMEM((tm, D), BF16),
                            pltpu.SemaphoreType.DMA((2,))]),
        out_shape=jax.ShapeDtypeStruct((n_rows, D), F32),
        compiler_params=_cparams(2),
        name="moe_experts",
    )(block_e, n_used, row_tok, x2, g, w_gate, w_up, w_down)


def _combine_body(dest_ref, yg_hbm, x_ref, tw_ref, g_ref, o_ref, ybuf, sem, *, tt, n_steps, final_norm):
    i = pl.program_id(0)

    def gather(blk, slot):
        def issue(r, c):
            for k in range(TOP_K):
                pltpu.make_async_copy(yg_hbm.at[pl.ds(dest_ref[(blk * tt + r) * TOP_K + k], 1), :],
                                      ybuf.at[slot, k, pl.ds(r, 1), :], sem.at[slot]).start()
            return c
        lax.fori_loop(0, tt, issue, 0)

    @pl.when(i == 0)
    def _():
        gather(0, 0)

    slot = i % 2

    @pl.when(i + 1 < n_steps)
    def _():
        gather(i + 1, 1 - slot)

    def wait(r, c):
        for k in range(TOP_K):
            pltpu.make_async_copy(yg_hbm.at[pl.ds(0, 1), :], ybuf.at[slot, k, pl.ds(r, 1), :],
                                  sem.at[slot]).wait()
        return c
    lax.fori_loop(0, tt, wait, 0)

    tw = tw_ref[...]
    y = x_ref[...] + ybuf[slot, 0] * tw[:, 0:1] + ybuf[slot, 1] * tw[:, 1:2]
    if final_norm:
        y = _rms(y, g_ref[...])
    o_ref[...] = y


def moe_combine(yg, x2, tw_col, dest, g_final, tt, final_norm):
    T, D = x2.shape
    n_steps = T // tt
    body = functools.partial(_combine_body, tt=tt, n_steps=n_steps, final_norm=final_norm)
    return pl.pallas_call(
        body,
        grid_spec=pltpu.PrefetchScalarGridSpec(
            num_scalar_prefetch=1,
            grid=(n_steps,),
            in_specs=[pl.BlockSpec(memory_space=pl.ANY),
                      pl.BlockSpec((tt, D), lambda i, d: (i, 0)),
                      pl.BlockSpec((tt, LANES), lambda i, d: (i, 0)),
                      pl.BlockSpec((1, D), lambda i, d: (0, 0))],
            out_specs=pl.BlockSpec((tt, D), lambda i, d: (i, 0)),
            scratch_shapes=[pltpu.VMEM((2, TOP_K, tt, D), F32),
                            pltpu.SemaphoreType.DMA((2,))]),
        out_shape=jax.ShapeDtypeStruct((T, D), F32),
        compiler_params=_cparams(1),
        name="moe_combine",
    )(dest, yg, x2, tw_col, g_final)


def moe_plan(top_e, tm):
    T = top_e.shape[1]
    TK = T * TOP_K
    flat_e = top_e.T.reshape(TK)
    onehot = (flat_e[:, None] == jnp.arange(N_EXPERTS, dtype=jnp.int32)[None, :]).astype(jnp.int32)
    cum = jnp.cumsum(onehot, axis=0)
    counts = cum[-1]
    rank = jnp.sum(onehot * (cum - 1), axis=1)
    padded = (counts + tm - 1) // tm * tm
    pad_end = jnp.cumsum(padded)
    pad_start = pad_end - padded
    dest = jnp.sum(onehot * pad_start[None, :], axis=1) + rank
    n_rows = TK + N_EXPERTS * tm
    n_blocks = n_rows // tm
    row_tok = jnp.zeros((n_rows,), jnp.int32).at[dest].set(jnp.arange(TK, dtype=jnp.int32) // TOP_K)
    block_e = jnp.minimum(jnp.searchsorted(pad_end, jnp.arange(n_blocks, dtype=jnp.int32) * tm, side='right'),
                          N_EXPERTS - 1).astype(jnp.int32)
    n_used = (pad_end[-1:] // tm).astype(jnp.int32)
    return dest.astype(jnp.int32), row_tok, block_e, n_used


def moe_block(x2, g, router_w, w_gate, w_up, w_down, g_final, final_norm, tm_r, tm_e, tf, tt):
    top_e, top_w = moe_router(x2, g, router_w.T.astype(BF16), tm_r)
    dest, row_tok, block_e, n_used = moe_plan(top_e, tm_e)
    yg = moe_experts(x2, g, w_gate, w_up, w_down, block_e, n_used, row_tok, tm_e, tf)
    tw_col = jnp.pad(top_w.T, ((0, 0), (0, LANES - TOP_K)))
    return moe_combine(yg, x2, tw_col, dest, g_final, tt, final_norm)


def _norm_body(x_ref, g_ref, o_ref):
    o_ref[...] = _rms(x_ref[...], g_ref[...])


def final_norm(x2, g, tm):
    T, D = x2.shape
    return pl.pallas_call(
        _norm_body,
        grid=(T // tm,),
        in_specs=[pl.BlockSpec((tm, D), lambda i: (i, 0)), pl.BlockSpec((1, D), lambda i: (0, 0))],
        out_specs=pl.BlockSpec((tm, D), lambda i: (i, 0)),
        out_shape=jax.ShapeDtypeStruct((T, D), F32),
        compiler_params=_cparams(1),
        name="final_norm",
    )(x2, g)


def _split_w_in(w):
    sizes = (GDN_KEY_DIM, GDN_KEY_DIM, GDN_VAL_DIM, GDN_VAL_DIM, GDN_V_HEADS, GDN_V_HEADS,
             SWA_Q_DIM, SWA_KV_DIM, SWA_KV_DIM, D_MODEL, D_MODEL)
    offs = [0]
    for s in sizes:
        offs.append(offs[-1] + s)
    q, k, v, z, b, a, sq, sk, sv, ga, gb = [w[:, offs[n]:offs[n + 1]] for n in range(len(sizes))]
    w_main = jnp.concatenate([q, k, v, z, sq, ga, gb, sk, sv], axis=1).astype(BF16)
    w_ba = jnp.pad(jnp.concatenate([b, a], axis=1), ((0, 0), (0, LANES - 2 * GDN_V_HEADS))).astype(BF16)
    return w_main, w_ba


def _pad_lanes(vec, offset):
    return jnp.pad(vec.astype(F32), (offset, LANES - offset - vec.shape[0])).reshape(1, LANES)


def mixer_layer(x2, batch, seq, ln_g, w_in, conv_w, a_log, dt_bias, gdn_norm_g, bias_tbl, sink_tbl,
                w_proj_gdn, w_proj_swa, w_out):
    T = x2.shape[0]
    w_main, w_ba = _split_w_in(w_in)
    proj, ba = in_proj(x2, ln_g.reshape(1, -1), w_main, w_ba, tm=1024, tn=1280)
    qn, kn, vv, gb = gdn_prep(proj, ba, conv_w, _pad_lanes(a_log, GDN_V_HEADS),
                              _pad_lanes(dt_bias, GDN_V_HEADS), seq, ts=256)
    n_chunks = seq // GDN_CHUNK
    gcr = gb[:, GDN_V_HEADS:2 * GDN_V_HEADS].reshape(batch, n_chunks, GDN_CHUNK, GDN_V_HEADS)
    gcr = gcr.transpose(0, 3, 1, 2)
    o_a = gdn_core(qn, kn, vv, proj, gb, gcr, gdn_norm_g.reshape(1, -1).astype(F32), batch, seq, sb=512)
    o_b = swa_core(proj, bias_tbl, sink_tbl, batch, seq)
    y = merge_proj(o_a, o_b, w_proj_gdn.astype(BF16), w_proj_swa.astype(BF16), proj, tm=1024, tn=512)
    return out_proj(y, w_out.astype(BF16), x2, tm=1024, tn=512)


def kernel(x, ln_mix_g, w_in, conv_w, a_log, dt_bias, gdn_norm_g, sinks, rel_bias, w_proj_gdn, w_proj_swa,
           w_out, ln_ffn_g, ffn_w_gate, ffn_w_up, ffn_w_down, router_w, moe_w_gate, moe_w_up, moe_w_down,
           ln_final_g):
    B, S, D = x.shape
    depth = w_in.shape[0]
    x2 = x.reshape(B * S, D)
    g_final = ln_final_g.reshape(1, D)
    normed = False
    for i in range(depth):
        bias_tbl, sink_tbl = swa_tables(rel_bias, sinks[i])
        x2 = mixer_layer(x2, B, S, ln_mix_g[i], w_in[i], conv_w[i], a_log[i], dt_bias[i], gdn_norm_g[i],
                         bias_tbl, sink_tbl, w_proj_gdn[i], w_proj_swa[i], w_out[i])
        g_ffn = ln_ffn_g[i].reshape(1, D)
        j = i // 2
        if i % 2 == 0:
            f = ffn_w_gate.shape[2]
            fpad = (-f) % 512
            wg = jnp.pad(ffn_w_gate[j], ((0, 0), (0, fpad))).astype(BF16)
            wu = jnp.pad(ffn_w_up[j], ((0, 0), (0, fpad))).astype(BF16)
            wd = jnp.pad(ffn_w_down[j], ((0, fpad), (0, 0))).astype(BF16)
            x2 = dense_ffn(x2, g_ffn, wg, wu, wd, tm=512, tf=512)
        else:
            last = i == depth - 1
            x2 = moe_block(x2, g_ffn, router_w[j], moe_w_gate[j].astype(BF16), moe_w_up[j].astype(BF16),
                           moe_w_down[j].astype(BF16), g_final, last, tm_r=512, tm_e=512, tf=512, tt=256)
            normed = last
    if not normed:
        x2 = final_norm(x2, g_final, tm=512)
    return x2.reshape(B, S, D)
```

```python
import functools
import math

import jax
import jax.numpy as jnp
from jax import lax
from jax.experimental import pallas as pl
from jax.experimental.pallas import tpu as pltpu

F32 = jnp.float32
BF16 = jnp.bfloat16

D_MODEL = 2048
GDN_QK_HEADS = 8
GDN_V_HEADS = 16
GDN_HEAD_DIM = 128
GDN_CONV = 4
GDN_CHUNK = 64
SWA_Q_HEADS = 32
SWA_KV_HEADS = 4
SWA_HEAD_DIM = 64
SWA_WINDOW = 128
REL_BUCKETS = 32
REL_MAX_DIST = 128
N_EXPERTS = 8
TOP_K = 2
NORM_EPS = 1e-6
GDN_KEY_DIM = GDN_QK_HEADS * GDN_HEAD_DIM
GDN_VAL_DIM = GDN_V_HEADS * GDN_HEAD_DIM
SWA_Q_DIM = SWA_Q_HEADS * SWA_HEAD_DIM
SWA_KV_DIM = SWA_KV_HEADS * SWA_HEAD_DIM

LANES = 128
VMEM_LIMIT = 56 * 1024 * 1024

COL_Q = 0
COL_K = COL_Q + GDN_KEY_DIM
COL_V = COL_K + GDN_KEY_DIM
COL_Z = COL_V + GDN_VAL_DIM
COL_SQ = COL_Z + GDN_VAL_DIM
COL_GA = COL_SQ + SWA_Q_DIM
COL_GB = COL_GA + D_MODEL
COL_SK = COL_GB + D_MODEL
COL_SV = COL_SK + SWA_KV_DIM
PROJ_COLS = COL_SV + SWA_KV_DIM
GDN_CONV_DIM = 2 * GDN_KEY_DIM + GDN_VAL_DIM

MASKED = -1e30
DMA_UNROLL = 8


def _cparams(n_axes):
    return pltpu.CompilerParams(dimension_semantics=("arbitrary",) * n_axes,
                                vmem_limit_bytes=VMEM_LIMIT)


def _rms(x, g):
    return x * lax.rsqrt(jnp.mean(x * x, axis=-1, keepdims=True) + NORM_EPS) * g


def _silu(x):
    return x * jax.nn.sigmoid(x)


def _dot(a, b):
    return jnp.dot(a, b, preferred_element_type=F32)


def _dot_nt(a, b):
    return lax.dot_general(a, b, (((1,), (1,)), ((), ())), preferred_element_type=F32)


def _dot_tn(a, b):
    return lax.dot_general(a, b, (((0,), (0,)), ((), ())), preferred_element_type=F32)


def _inproj_body(x_ref, g_ref, w_ref, wba_ref, o_ref, oba_ref, h_ref):
    @pl.when(pl.program_id(1) == 0)
    def _():
        h = _rms(x_ref[...], g_ref[...]).astype(BF16)
        h_ref[...] = h
        oba_ref[...] = _dot(h, wba_ref[...])

    o_ref[...] = _dot(h_ref[...], w_ref[...]).astype(o_ref.dtype)


def in_proj(x2, g, w_main, w_ba, tm, tn):
    T, D = x2.shape
    N = w_main.shape[1]
    return pl.pallas_call(
        _inproj_body,
        grid=(T // tm, N // tn),
        in_specs=[pl.BlockSpec((tm, D), lambda i, j: (i, 0)),
                  pl.BlockSpec((1, D), lambda i, j: (0, 0)),
                  pl.BlockSpec((D, tn), lambda i, j: (0, j)),
                  pl.BlockSpec((D, LANES), lambda i, j: (0, 0))],
        out_specs=[pl.BlockSpec((tm, tn), lambda i, j: (i, j)),
                   pl.BlockSpec((tm, LANES), lambda i, j: (i, 0))],
        out_shape=[jax.ShapeDtypeStruct((T, N), BF16), jax.ShapeDtypeStruct((T, LANES), F32)],
        scratch_shapes=[pltpu.VMEM((tm, D), BF16)],
        compiler_params=_cparams(2),
        name="in_proj",
    )(x2, g, w_main, w_ba)


def _prep_body(x_ref, halo_ref, cw_ref, ba_ref, alog_ref, dtb_ref, q_ref, k_ref, v_ref, gb_ref,
               *, ts, seq, n_qk, n_v, chunk):
    first = (pl.program_id(0) * ts) % seq == 0
    for c in range(2 * n_qk + n_v):
        cs = slice(c * LANES, (c + 1) * LANES)
        cur = x_ref[:, cs].astype(F32)
        hal = halo_ref[8:16, cs].astype(F32)
        hal = jnp.where(first, 0.0, hal)
        ext = jnp.concatenate([hal, cur], axis=0)
        y = ext[5:5 + ts] * cw_ref[0:1, cs]
        y = y + ext[6:6 + ts] * cw_ref[1:2, cs]
        y = y + ext[7:7 + ts] * cw_ref[2:3, cs]
        y = y + cur * cw_ref[3:4, cs]
        y = _silu(y)
        if c < 2 * n_qk:
            y = y * lax.rsqrt(jnp.sum(y * y, axis=-1, keepdims=True) + 1e-6)
        if c < n_qk:
            q_ref[:, cs] = (y * (GDN_HEAD_DIM ** -0.5)).astype(q_ref.dtype)
        elif c < 2 * n_qk:
            k_ref[:, (c - n_qk) * LANES:(c - n_qk + 1) * LANES] = y.astype(k_ref.dtype)
        else:
            v_ref[:, (c - 2 * n_qk) * LANES:(c - 2 * n_qk + 1) * LANES] = y.astype(v_ref.dtype)

    ba = ba_ref[...]
    lane = lax.broadcasted_iota(jnp.int32, ba.shape, 1)
    pos = lax.broadcasted_iota(jnp.int32, ba.shape, 0) % chunk
    beta = jax.nn.sigmoid(ba)
    t = ba + dtb_ref[...]
    softplus = jnp.maximum(t, 0.0) + jnp.log1p(jnp.exp(-jnp.abs(t)))
    gc = -jnp.exp(alog_ref[...]) * softplus
    s = 1
    while s < chunk:
        gc = gc + jnp.where(pos >= s, pltpu.roll(gc, s, 0), 0.0)
        s *= 2
    gb_ref[...] = jnp.where(lane < n_v, beta, jnp.where(lane < 2 * n_v, gc, 0.0))


def gdn_prep(proj, ba, conv_w, alog_pad, dtb_pad, seq, ts):
    T = proj.shape[0]
    n_qk, n_v = GDN_QK_HEADS, GDN_V_HEADS
    body = functools.partial(_prep_body, ts=ts, seq=seq, n_qk=n_qk, n_v=n_v, chunk=GDN_CHUNK)
    return pl.pallas_call(
        body,
        grid=(T // ts,),
        in_specs=[pl.BlockSpec((ts, GDN_CONV_DIM), lambda i: (i, 0)),
                  pl.BlockSpec((16, GDN_CONV_DIM), lambda i: (jnp.maximum(i * (ts // 16) - 1, 0), 0)),
                  pl.BlockSpec((GDN_CONV, GDN_CONV_DIM), lambda i: (0, 0)),
                  pl.BlockSpec((ts, LANES), lambda i: (i, 0)),
                  pl.BlockSpec((1, LANES), lambda i: (0, 0)),
                  pl.BlockSpec((1, LANES), lambda i: (0, 0))],
        out_specs=[pl.BlockSpec((ts, GDN_KEY_DIM), lambda i: (i, 0)),
                   pl.BlockSpec((ts, GDN_KEY_DIM), lambda i: (i, 0)),
                   pl.BlockSpec((ts, GDN_VAL_DIM), lambda i: (i, 0)),
                   pl.BlockSpec((ts, LANES), lambda i: (i, 0))],
        out_shape=[jax.ShapeDtypeStruct((T, GDN_KEY_DIM), BF16),
                   jax.ShapeDtypeStruct((T, GDN_KEY_DIM), BF16),
                   jax.ShapeDtypeStruct((T, GDN_VAL_DIM), BF16),
                   jax.ShapeDtypeStruct((T, LANES), F32)],
        compiler_params=_cparams(1),
        name="gdn_prep",
    )(proj, proj, conv_w, ba, alog_pad, dtb_pad)


def _gdn_body(q_ref, k_ref, v_ref, z_ref, gb_ref, gcr_ref, gn_ref, o_ref,
              u_s, wq_s, qkd_s, gl_s, st_s, *, n_chunks, n_qk, n_v):
    C = GDN_CHUNK
    hd = GDN_HEAD_DIM
    rep = n_v // n_qk

    @pl.when(pl.program_id(1) == 0)
    def _():
        st_s[...] = jnp.zeros_like(st_s)

    row = lax.broadcasted_iota(jnp.int32, (C, C), 0)
    col = lax.broadcasted_iota(jnp.int32, (C, C), 1)
    causal = row >= col
    strict = row > col

    hs = range(n_qk)
    hvs = range(n_v)

    def phase1(c, carry):
        rows = pl.ds(pl.multiple_of(c * C, C), C)
        rows2 = pl.ds(pl.multiple_of(c * 2 * C, 2 * C), C)
        rows2b = pl.ds(pl.multiple_of(c * 2 * C, 2 * C) + C, C)
        gbc = gb_ref[rows, :]
        q = [q_ref[rows, h * hd:(h + 1) * hd] for h in hs]
        k = [k_ref[rows, h * hd:(h + 1) * hd] for h in hs]
        kk = [_dot_nt(k[h], k[h]) for h in hs]
        qk = [_dot_nt(q[h], k[h]) for h in hs]
        bcol = [gbc[:, hv:hv + 1] for hv in hvs]
        gcol = [gbc[:, n_v + hv:n_v + hv + 1] for hv in hvs]
        glast = [gbc[C - 1:C, n_v + hv:n_v + hv + 1] for hv in hvs]
        decay = [jnp.exp(jnp.where(causal, gcol[hv] - gcr_ref[0, hv, pl.ds(c, 1), :], -jnp.inf)) for hv in hvs]
        bk = [jnp.where(strict, kk[hv // rep] * bcol[hv] * decay[hv], 0.0) * -1.0 for hv in hvs]
        nn = list(bk)
        egc = None
        p = 1
        while 2 * p < C:
            bb = [x.astype(BF16) for x in bk]
            bk = [_dot(x, x) for x in bb]
            if egc is None:
                egc = [jnp.exp(g) for g in gcol]
                for hv in hvs:
                    qf = q[hv // rep].astype(F32)
                    kf = k[hv // rep].astype(F32)
                    wq_s[hv, rows2b, :] = (qf * egc[hv]).astype(BF16)
                    kd = kf * jnp.exp(glast[hv] - gcol[hv])
                    qkd_s[hv, c, 0:C, :] = (qk[hv // rep] * decay[hv]).astype(BF16)
                    qkd_s[hv, c, C:3 * C, :] = kd.T.astype(BF16)
                    gl_s[hv, pl.ds(c, 1), :] = jnp.broadcast_to(jnp.exp(glast[hv]), (1, LANES))
            nn = [n + b + _dot(n.astype(BF16), b.astype(BF16)) for n, b in zip(nn, bk)]
            p *= 2
        nb = [n.astype(BF16) for n in nn]
        vb = [v_ref[rows, hv * hd:(hv + 1) * hd].astype(F32) * bcol[hv] for hv in hvs]
        kbg = [k[hv // rep].astype(F32) * (bcol[hv] * egc[hv]) for hv in hvs]
        uu = [_dot(nb[hv], vb[hv].astype(BF16)) for hv in hvs]
        ww = [_dot(nb[hv], kbg[hv].astype(BF16)) for hv in hvs]
        for hv in hvs:
            u_s[hv, rows, :] = vb[hv] + uu[hv]
            wq_s[hv, rows2, :] = (kbg[hv] + ww[hv]).astype(BF16)
        return carry

    lax.fori_loop(0, n_chunks, phase1, 0)

    def phase2(c, carry):
        rows = pl.ds(pl.multiple_of(c * C, C), C)
        rows_wq = pl.ds(pl.multiple_of(c * 2 * C, 2 * C), 2 * C)
        sb = [st_s[hv].astype(BF16) for hv in hvs]
        ws = [_dot(wq_s[hv, rows_wq, :], sb[hv]) for hv in hvs]
        vnb = [(u_s[hv, rows, :] - ws[hv][0:C]).astype(BF16) for hv in hvs]
        ov = [_dot(qkd_s[hv, c], vnb[hv]) for hv in hvs]
        for hv in hvs:
            st_s[hv] = st_s[hv] * gl_s[hv, pl.ds(c, 1), :] + ov[hv][C:3 * C]
        for hv in hvs:
            o = ws[hv][C:2 * C] + ov[hv][0:C]
            o = o * lax.rsqrt(jnp.mean(o * o, axis=-1, keepdims=True) + NORM_EPS) * gn_ref[...]
            o = o * _silu(z_ref[rows, hv * hd:(hv + 1) * hd].astype(F32))
            o_ref[rows, hv * hd:(hv + 1) * hd] = o.astype(o_ref.dtype)
        return carry

    lax.fori_loop(0, n_chunks, phase2, 0)


def gdn_core(qn, kn, vv, proj, gb, gcr, gnorm, batch, seq, sb):
    T = qn.shape[0]
    n_qk, n_v, hd, C = GDN_QK_HEADS, GDN_V_HEADS, GDN_HEAD_DIM, GDN_CHUNK
    nc = sb // C
    nsb = seq // sb
    zc = COL_Z // (n_v * hd)
    body = functools.partial(_gdn_body, n_chunks=nc, n_qk=n_qk, n_v=n_v)
    blk = lambda b, s: (b * nsb + s, 0)
    return pl.pallas_call(
        body,
        grid=(batch, nsb),
        in_specs=[pl.BlockSpec((sb, n_qk * hd), blk),
                  pl.BlockSpec((sb, n_qk * hd), blk),
                  pl.BlockSpec((sb, n_v * hd), blk),
                  pl.BlockSpec((sb, n_v * hd), lambda b, s: (b * nsb + s, zc)),
                  pl.BlockSpec((sb, LANES), blk),
                  pl.BlockSpec((1, n_v, nc, C), lambda b, s: (b, 0, s, 0)),
                  pl.BlockSpec((1, hd), lambda b, s: (0, 0))],
        out_specs=pl.BlockSpec((sb, n_v * hd), blk),
        out_shape=jax.ShapeDtypeStruct((T, n_v * hd), BF16),
        scratch_shapes=[pltpu.VMEM((n_v, sb, hd), F32),
                        pltpu.VMEM((n_v, 2 * sb, hd), BF16),
                        pltpu.VMEM((n_v, nc, 3 * C, C), BF16),
                        pltpu.VMEM((n_v, nc, LANES), F32),
                        pltpu.VMEM((n_v, hd, hd), F32)],
        compiler_params=_cparams(2),
        name="gdn_core",
    )(qn, kn, vv, proj, gb, gcr, gnorm)


def _swa_body(q_ref, kp_ref, kc_ref, vp_ref, vc_ref, bias_ref, sink_ref, o_ref):
    W, dh = SWA_WINDOW, SWA_HEAD_DIM
    pairs = (SWA_Q_HEADS // SWA_KV_HEADS) // 2
    kvs = range(SWA_KV_HEADS)
    zero = jnp.zeros((2 * W, dh), BF16)
    one = jnp.ones((2 * W, dh), BF16)
    low = lax.broadcasted_iota(jnp.int32, (1, LANES), 1) < dh

    logits = []
    for hk in kvs:
        ks = slice(hk * dh, (hk + 1) * dh)
        kb = jnp.concatenate([kp_ref[:, ks], kc_ref[:, ks]], axis=0)
        krhs = jnp.concatenate([jnp.concatenate([kb, zero], axis=1),
                                jnp.concatenate([zero, kb], axis=1)], axis=0)
        qs = jnp.concatenate([q_ref[:, (hk * pairs + p) * LANES:(hk * pairs + p + 1) * LANES]
                              for p in range(pairs)], axis=0)
        qs = qs * jnp.asarray(dh ** -0.5, BF16)
        logits.append(_dot_nt(qs, krhs) + bias_ref[0, hk])

    outs, sinks = [], []
    for hk in kvs:
        ks = slice(hk * dh, (hk + 1) * dh)
        vb = jnp.concatenate([vp_ref[:, ks], vc_ref[:, ks]], axis=0)
        vrhs = jnp.concatenate([jnp.concatenate([vb, zero, one, zero], axis=1),
                                jnp.concatenate([zero, vb, zero, one], axis=1)], axis=0)
        ps, ms = [], []
        for c in range(2):
            lg = logits[hk][:, c * 2 * W:(c + 1) * 2 * W]
            sk = sink_ref[hk, c]
            m = jnp.max(jnp.maximum(jnp.maximum(lg[:, :W], lg[:, W:]), sk), axis=-1, keepdims=True)
            ps.append(jnp.exp(lg - m).astype(BF16))
            ms.append(sk - m)
        outs.append(_dot(jnp.concatenate(ps, axis=1), vrhs))
        sinks.append(jnp.exp(jnp.where(low, ms[0], ms[1])))

    for hk in kvs:
        out = outs[hk][:, :LANES] / (outs[hk][:, LANES:] + sinks[hk])
        for p in range(pairs):
            o_ref[:, (hk * pairs + p) * LANES:(hk * pairs + p + 1) * LANES] = (
                out[p * W:(p + 1) * W].astype(o_ref.dtype))


def swa_core(proj, bias_tbl, sink_tbl, batch, seq):
    T = proj.shape[0]
    W = SWA_WINDOW
    nb = seq // W
    qc, kc, vc = COL_SQ // SWA_Q_DIM, COL_SK // SWA_KV_DIM, COL_SV // SWA_KV_DIM
    cur = lambda b, n: b * nb + n
    prev = lambda b, n: jnp.maximum(b * nb + n - 1, 0)
    return pl.pallas_call(
        _swa_body,
        grid=(batch, nb),
        in_specs=[pl.BlockSpec((W, SWA_Q_DIM), lambda b, n: (cur(b, n), qc)),
                  pl.BlockSpec((W, SWA_KV_DIM), lambda b, n: (prev(b, n), kc)),
                  pl.BlockSpec((W, SWA_KV_DIM), lambda b, n: (cur(b, n), kc)),
                  pl.BlockSpec((W, SWA_KV_DIM), lambda b, n: (prev(b, n), vc)),
                  pl.BlockSpec((W, SWA_KV_DIM), lambda b, n: (cur(b, n), vc)),
                  pl.BlockSpec((1,) + bias_tbl.shape[1:], lambda b, n: (jnp.minimum(n, 1), 0, 0, 0)),
                  pl.BlockSpec(sink_tbl.shape, lambda b, n: (0, 0, 0, 0))],
        out_specs=pl.BlockSpec((W, SWA_Q_DIM), lambda b, n: (cur(b, n), 0)),
        out_shape=jax.ShapeDtypeStruct((T, SWA_Q_DIM), BF16),
        compiler_params=_cparams(2),
        name="swa_core",
    )(proj, proj, proj, proj, proj, bias_tbl, sink_tbl)


def _t5_bucket(dist):
    max_exact = REL_BUCKETS // 2
    large = max_exact + (jnp.log(jnp.maximum(dist, 1).astype(F32) / max_exact)
                         / math.log(REL_MAX_DIST / max_exact) * (REL_BUCKETS - max_exact)).astype(jnp.int32)
    large = jnp.minimum(large, REL_BUCKETS - 1)
    return jnp.where(dist < max_exact, dist, large)


def swa_tables(rel_bias, sinks):
    W, hq, hkv = SWA_WINDOW, SWA_Q_HEADS, SWA_KV_HEADS
    pairs = (hq // hkv) // 2
    by_dist = rel_bias.astype(F32)[_t5_bucket(jnp.arange(W))]
    period = 3 * W
    vec = jnp.concatenate([jnp.full((1, hq), MASKED, F32), by_dist[::-1],
                           jnp.full((period - W - 1, hq), MASKED, F32)], axis=0).T
    tbl = jnp.tile(vec, (1, W))[:, :W * (period - 1)].reshape(hq, W, period - 1)[:, :, :2 * W]
    first = jnp.where(jnp.arange(2 * W)[None, None, :] < W, MASKED, tbl)
    tbl = jnp.stack([first, tbl])
    tbl = tbl.reshape(2, hkv, pairs, 2, W, 2 * W).transpose(0, 1, 2, 4, 3, 5)
    tbl = tbl.reshape(2, hkv, pairs * W, 4 * W)
    sk = sinks.astype(F32).reshape(hkv, pairs, 2).transpose(0, 2, 1)
    sk = jnp.broadcast_to(sk[:, :, :, None, None], (hkv, 2, pairs, W, LANES)).reshape(hkv, 2, pairs * W, LANES)
    return tbl, sk


def _merge_body(oa_ref, ob_ref, wa_ref, wb_ref, ga_ref, gb_ref, y_ref):
    a = _dot(oa_ref[...], wa_ref[...])
    b = _dot(ob_ref[...], wb_ref[...])
    y = jax.nn.sigmoid(ga_ref[...].astype(F32)) * a + jax.nn.sigmoid(gb_ref[...].astype(F32)) * b
    y_ref[...] = y.astype(y_ref.dtype)


def merge_proj(o_a, o_b, w_a, w_b, proj, tm, tn):
    T, K = o_a.shape
    N = w_a.shape[1]
    ga0, gb0 = COL_GA // tn, COL_GB // tn
    return pl.pallas_call(
        _merge_body,
        grid=(T // tm, N // tn),
        in_specs=[pl.BlockSpec((tm, K), lambda i, j: (i, 0)),
                  pl.BlockSpec((tm, K), lambda i, j: (i, 0)),
                  pl.BlockSpec((K, tn), lambda i, j: (0, j)),
                  pl.BlockSpec((K, tn), lambda i, j: (0, j)),
                  pl.BlockSpec((tm, tn), lambda i, j: (i, ga0 + j)),
                  pl.BlockSpec((tm, tn), lambda i, j: (i, gb0 + j))],
        out_specs=pl.BlockSpec((tm, tn), lambda i, j: (i, j)),
        out_shape=jax.ShapeDtypeStruct((T, N), BF16),
        compiler_params=_cparams(2),
        name="merge_proj",
    )(o_a, o_b, w_a, w_b, proj, proj)


def _resid_body(y_ref, w_ref, x_ref, o_ref):
    o_ref[...] = x_ref[...] + _dot(y_ref[...], w_ref[...])


def out_proj(y, w, x2, tm, tn):
    T, K = y.shape
    N = w.shape[1]
    return pl.pallas_call(
        _resid_body,
        grid=(T // tm, N // tn),
        in_specs=[pl.BlockSpec((tm, K), lambda i, j: (i, 0)),
                  pl.BlockSpec((K, tn), lambda i, j: (0, j)),
                  pl.BlockSpec((tm, tn), lambda i, j: (i, j))],
        out_specs=pl.BlockSpec((tm, tn), lambda i, j: (i, j)),
        out_shape=jax.ShapeDtypeStruct((T, N), F32),
        compiler_params=_cparams(2),
        name="out_proj",
    )(y, w, x2)


def _ffn_body(x_ref, g_ref, wg_ref, wu_ref, wd_ref, o_ref, h_ref):
    @pl.when(pl.program_id(1) == 0)
    def _():
        x = x_ref[...]
        h_ref[...] = _rms(x, g_ref[...]).astype(BF16)
        o_ref[...] = x

    h = h_ref[...]
    mid = (_silu(_dot(h, wg_ref[...])) * _dot(h, wu_ref[...])).astype(BF16)
    o_ref[...] += _dot(mid, wd_ref[...])


def dense_ffn(x2, g, w_gate, w_up, w_down, tm, tf):
    T, D = x2.shape
    F = w_gate.shape[1]
    return pl.pallas_call(
        _ffn_body,
        grid=(T // tm, F // tf),
        in_specs=[pl.BlockSpec((tm, D), lambda i, f: (i, 0)),
                  pl.BlockSpec((1, D), lambda i, f: (0, 0)),
                  pl.BlockSpec((D, tf), lambda i, f: (0, f)),
                  pl.BlockSpec((D, tf), lambda i, f: (0, f)),
                  pl.BlockSpec((tf, D), lambda i, f: (f, 0))],
        out_specs=pl.BlockSpec((tm, D), lambda i, f: (i, 0)),
        out_shape=jax.ShapeDtypeStruct((T, D), F32),
        scratch_shapes=[pltpu.VMEM((tm, D), BF16)],
        compiler_params=_cparams(2),
        name="dense_ffn",
    )(x2, g, w_gate, w_up, w_down)


def _router_body(x_ref, g_ref, rwt_ref, e_ref, w_ref):
    h = _rms(x_ref[...], g_ref[...]).astype(BF16)
    lt = _dot_nt(rwt_ref[...], h)
    idx = lax.broadcasted_iota(jnp.int32, lt.shape, 0)
    m1 = jnp.max(lt, axis=0, keepdims=True)
    i1 = jnp.min(jnp.where(lt == m1, idx, N_EXPERTS), axis=0, keepdims=True)
    lt2 = jnp.where(idx == i1, -jnp.inf, lt)
    m2 = jnp.max(lt2, axis=0, keepdims=True)
    i2 = jnp.min(jnp.where(lt2 == m2, idx, N_EXPERTS), axis=0, keepdims=True)
    e2 = jnp.exp(m2 - m1)
    den = 1.0 + e2
    e_ref[...] = jnp.concatenate([i1, i2], axis=0)
    w_ref[...] = jnp.concatenate([1.0 / den, e2 / den], axis=0)


def moe_router(x2, g, rw_t, tm):
    T, D = x2.shape
    return pl.pallas_call(
        _router_body,
        grid=(T // tm,),
        in_specs=[pl.BlockSpec((tm, D), lambda i: (i, 0)),
                  pl.BlockSpec((1, D), lambda i: (0, 0)),
                  pl.BlockSpec((N_EXPERTS, D), lambda i: (0, 0))],
        out_specs=[pl.BlockSpec((TOP_K, tm), lambda i: (0, i)),
                   pl.BlockSpec((TOP_K, tm), lambda i: (0, i))],
        out_shape=[jax.ShapeDtypeStruct((TOP_K, T), jnp.int32),
                   jax.ShapeDtypeStruct((TOP_K, T), F32)],
        compiler_params=_cparams(1),
        name="moe_router",
    )(x2, g, rw_t)


def _row_copy(src_hbm, idx, buf, slot, r, sem):
    return pltpu.make_async_copy(src_hbm.at[pl.ds(idx, 1), :], buf.at[slot, pl.ds(r, 1), :], sem.at[slot])


def _moe_body(be_ref, nu_ref, tok_ref, x_hbm, g_ref, wg_ref, wu_ref, wd_ref, o_ref,
              xbuf, h_ref, sem, *, tm):
    i = pl.program_id(0)
    f = pl.program_id(1)
    n_used = nu_ref[0]

    def gather(blk, slot):
        def issue(r8, c):
            for u in range(DMA_UNROLL):
                r = r8 * DMA_UNROLL + u
                _row_copy(x_hbm, tok_ref[blk * tm + r], xbuf, slot, r, sem).start()
            return c
        lax.fori_loop(0, tm // DMA_UNROLL, issue, 0)

    @pl.when((f == 0) & (i == 0))
    def _():
        gather(0, 0)

    @pl.when((f == 0) & (i < n_used))
    def _():
        slot = i % 2

        @pl.when(i + 1 < n_used)
        def _():
            gather(i + 1, 1 - slot)

        for r in range(tm):
            _row_copy(x_hbm, 0, xbuf, slot, r, sem).wait()
        h_ref[...] = _rms(xbuf[slot], g_ref[...]).astype(BF16)

    @pl.when(f == 0)
    def _():
        o_ref[...] = jnp.zeros_like(o_ref)

    @pl.when(i < n_used)
    def _():
        h = h_ref[...]
        mid = (_silu(_dot(h, wg_ref[...])) * _dot(h, wu_ref[...])).astype(BF16)
        o_ref[...] += _dot(mid, wd_ref[...])


def moe_experts(x2, g, w_gate, w_up, w_down, block_e, n_used, row_tok, tm, tf):
    T, D = x2.shape
    E, _, F = w_gate.shape
    n_rows = row_tok.shape[0]
    nf = F // tf

    def fidx(i, f, nu):
        return jnp.where(i < nu[0], f, nf - 1)

    body = functools.partial(_moe_body, tm=tm)
    return pl.pallas_call(
        body,
        grid_spec=pltpu.PrefetchScalarGridSpec(
            num_scalar_prefetch=3,
            grid=(n_rows // tm, nf),
            in_specs=[pl.BlockSpec(memory_space=pl.ANY),
                      pl.BlockSpec((1, D), lambda i, f, be, nu, tok: (0, 0)),
                      pl.BlockSpec((None, D, tf), lambda i, f, be, nu, tok: (be[i], 0, fidx(i, f, nu))),
                      pl.BlockSpec((None, D, tf), lambda i, f, be, nu, tok: (be[i], 0, fidx(i, f, nu))),
                      pl.BlockSpec((None, tf, D), lambda i, f, be, nu, tok: (be[i], fidx(i, f, nu), 0))],
            out_specs=pl.BlockSpec((tm, D), lambda i, f, be, nu, tok: (i, 0)),
            scratch_shapes=[pltpu.VMEM((2, tm, D), F32),
                            pltpu.VMEM((tm, D), BF16),
                            pltpu.SemaphoreType.DMA((2,))]),
        out_shape=jax.ShapeDtypeStruct((n_rows, D), F32),
        compiler_params=_cparams(2),
        name="moe_experts",
    )(block_e, n_used, row_tok, x2, g, w_gate, w_up, w_down)


def _combine_body(dest_ref, yg_hbm, x_ref, tw_ref, g_ref, o_ref, ybuf, sem, *, tt, n_steps, final_norm):
    i = pl.program_id(0)

    def gather(blk, slot):
        def issue(r8, c):
            for u in range(DMA_UNROLL):
                r = r8 * DMA_UNROLL + u
                for k in range(TOP_K):
                    pltpu.make_async_copy(yg_hbm.at[pl.ds(dest_ref[(blk * tt + r) * TOP_K + k], 1), :],
                                          ybuf.at[slot, k, pl.ds(r, 1), :], sem.at[slot]).start()
            return c
        lax.fori_loop(0, tt // DMA_UNROLL, issue, 0)

    @pl.when(i == 0)
    def _():
        gather(0, 0)

    slot = i % 2

    @pl.when(i + 1 < n_steps)
    def _():
        gather(i + 1, 1 - slot)

    for r in range(tt):
        for k in range(TOP_K):
            pltpu.make_async_copy(yg_hbm.at[pl.ds(0, 1), :], ybuf.at[slot, k, pl.ds(r, 1), :],
                                  sem.at[slot]).wait()

    tw = tw_ref[...]
    y = x_ref[...] + ybuf[slot, 0] * tw[:, 0:1] + ybuf[slot, 1] * tw[:, 1:2]
    if final_norm:
        y = _rms(y, g_ref[...])
    o_ref[...] = y


def moe_combine(yg, x2, tw_col, dest, g_final, tt, final_norm):
    T, D = x2.shape
    n_steps = T // tt
    body = functools.partial(_combine_body, tt=tt, n_steps=n_steps, final_norm=final_norm)
    return pl.pallas_call(
        body,
        grid_spec=pltpu.PrefetchScalarGridSpec(
            num_scalar_prefetch=1,
            grid=(n_steps,),
            in_specs=[pl.BlockSpec(memory_space=pl.ANY),
                      pl.BlockSpec((tt, D), lambda i, d: (i, 0)),
                      pl.BlockSpec((tt, LANES), lambda i, d: (i, 0)),
                      pl.BlockSpec((1, D), lambda i, d: (0, 0))],
            out_specs=pl.BlockSpec((tt, D), lambda i, d: (i, 0)),
            scratch_shapes=[pltpu.VMEM((2, TOP_K, tt, D), F32),
                            pltpu.SemaphoreType.DMA((2,))]),
        out_shape=jax.ShapeDtypeStruct((T, D), F32),
        compiler_params=_cparams(1),
        name="moe_combine",
    )(dest, yg, x2, tw_col, g_final)


def moe_plan(top_e, tm):
    T = top_e.shape[1]
    TK = T * TOP_K
    flat_e = top_e.T.reshape(TK)
    onehot = (flat_e[:, None] == jnp.arange(N_EXPERTS, dtype=jnp.int32)[None, :]).astype(jnp.int32)
    cum = jnp.cumsum(onehot, axis=0)
    counts = cum[-1]
    rank = jnp.sum(onehot * (cum - 1), axis=1)
    padded = (counts + tm - 1) // tm * tm
    pad_end = jnp.cumsum(padded)
    pad_start = pad_end - padded
    dest = jnp.sum(onehot * pad_start[None, :], axis=1) + rank
    n_rows = TK + N_EXPERTS * tm
    n_blocks = n_rows // tm
    row_tok = jnp.zeros((n_rows,), jnp.int32).at[dest].set(jnp.arange(TK, dtype=jnp.int32) // TOP_K)
    block_e = jnp.minimum(jnp.searchsorted(pad_end, jnp.arange(n_blocks, dtype=jnp.int32) * tm, side='right'),
                          N_EXPERTS - 1).astype(jnp.int32)
    n_used = (pad_end[-1:] // tm).astype(jnp.int32)
    return dest.astype(jnp.int32), row_tok, block_e, n_used


def moe_block(x2, g, router_w, w_gate, w_up, w_down, g_final, final_norm, tm_r, tm_e, tf, tt):
    top_e, top_w = moe_router(x2, g, router_w.T.astype(BF16), tm_r)
    dest, row_tok, block_e, n_used = moe_plan(top_e, tm_e)
    yg = moe_experts(x2, g, w_gate, w_up, w_down, block_e, n_used, row_tok, tm_e, tf)
    tw_col = jnp.pad(top_w.T, ((0, 0), (0, LANES - TOP_K)))
    return moe_combine(yg, x2, tw_col, dest, g_final, tt, final_norm)


def _norm_body(x_ref, g_ref, o_ref):
    o_ref[...] = _rms(x_ref[...], g_ref[...])


def final_norm(x2, g, tm):
    T, D = x2.shape
    return pl.pallas_call(
        _norm_body,
        grid=(T // tm,),
        in_specs=[pl.BlockSpec((tm, D), lambda i: (i, 0)), pl.BlockSpec((1, D), lambda i: (0, 0))],
        out_specs=pl.BlockSpec((tm, D), lambda i: (i, 0)),
        out_shape=jax.ShapeDtypeStruct((T, D), F32),
        compiler_params=_cparams(1),
        name="final_norm",
    )(x2, g)


def _split_w_in(w):
    sizes = (GDN_KEY_DIM, GDN_KEY_DIM, GDN_VAL_DIM, GDN_VAL_DIM, GDN_V_HEADS, GDN_V_HEADS,
             SWA_Q_DIM, SWA_KV_DIM, SWA_KV_DIM, D_MODEL, D_MODEL)
    offs = [0]
    for s in sizes:
        offs.append(offs[-1] + s)
    q, k, v, z, b, a, sq, sk, sv, ga, gb = [w[:, offs[n]:offs[n + 1]] for n in range(len(sizes))]
    w_main = jnp.concatenate([q, k, v, z, sq, ga, gb, sk, sv], axis=1).astype(BF16)
    w_ba = jnp.pad(jnp.concatenate([b, a], axis=1), ((0, 0), (0, LANES - 2 * GDN_V_HEADS))).astype(BF16)
    return w_main, w_ba


def _pad_lanes(vec, offset):
    return jnp.pad(vec.astype(F32), (offset, LANES - offset - vec.shape[0])).reshape(1, LANES)


def mixer_layer(x2, batch, seq, ln_g, w_in, conv_w, a_log, dt_bias, gdn_norm_g, bias_tbl, sink_tbl,
                w_proj_gdn, w_proj_swa, w_out):
    T = x2.shape[0]
    w_main, w_ba = _split_w_in(w_in)
    proj, ba = in_proj(x2, ln_g.reshape(1, -1), w_main, w_ba, tm=1024, tn=1280)
    qn, kn, vv, gb = gdn_prep(proj, ba, conv_w, _pad_lanes(a_log, GDN_V_HEADS),
                              _pad_lanes(dt_bias, GDN_V_HEADS), seq, ts=256)
    n_chunks = seq // GDN_CHUNK
    gcr = gb[:, GDN_V_HEADS:2 * GDN_V_HEADS].reshape(batch, n_chunks, GDN_CHUNK, GDN_V_HEADS)
    gcr = gcr.transpose(0, 3, 1, 2)
    o_a = gdn_core(qn, kn, vv, proj, gb, gcr, gdn_norm_g.reshape(1, -1).astype(F32), batch, seq, sb=512)
    o_b = swa_core(proj, bias_tbl, sink_tbl, batch, seq)
    y = merge_proj(o_a, o_b, w_proj_gdn.astype(BF16), w_proj_swa.astype(BF16), proj, tm=1024, tn=512)
    return out_proj(y, w_out.astype(BF16), x2, tm=1024, tn=512)


def kernel(x, ln_mix_g, w_in, conv_w, a_log, dt_bias, gdn_norm_g, sinks, rel_bias, w_proj_gdn, w_proj_swa,
           w_out, ln_ffn_g, ffn_w_gate, ffn_w_up, ffn_w_down, router_w, moe_w_gate, moe_w_up, moe_w_down,
           ln_final_g):
    B, S, D = x.shape
    depth = w_in.shape[0]
    x2 = x.reshape(B * S, D)
    g_final = ln_final_g.reshape(1, D)
    normed = False
    for i in range(depth):
        bias_tbl, sink_tbl = swa_tables(rel_bias, sinks[i])
        x2 = mixer_layer(x2, B, S, ln_mix_g[i], w_in[i], conv_w[i], a_log[i], dt_bias[i], gdn_norm_g[i],
                         bias_tbl, sink_tbl, w_proj_gdn[i], w_proj_swa[i], w_out[i])
        g_ffn = ln_ffn_g[i].reshape(1, D)
        j = i // 2
        if i % 2 == 0:
            f = ffn_w_gate.shape[2]
            fpad = (-f) % 512
            wg = jnp.pad(ffn_w_gate[j], ((0, 0), (0, fpad))).astype(BF16)
            wu = jnp.pad(ffn_w_up[j], ((0, 0), (0, fpad))).astype(BF16)
            wd = jnp.pad(ffn_w_down[j], ((0, fpad), (0, 0))).astype(BF16)
            x2 = dense_ffn(x2, g_ffn, wg, wu, wd, tm=512, tf=512)
        else:
            last = i == depth - 1
            x2 = moe_block(x2, g_ffn, router_w[j], moe_w_gate[j].astype(BF16), moe_w_up[j].astype(BF16),
                           moe_w_down[j].astype(BF16), g_final, last, tm_r=512, tm_e=512, tf=1024, tt=256)
            normed = last
    if not normed:
        x2 = final_norm(x2, g_final, tm=512)
    return x2.reshape(B, S, D)
```

```python
import functools
import math

import jax
import jax.numpy as jnp
from jax import lax
from jax.experimental import pallas as pl
from jax.experimental.pallas import tpu as pltpu

F32 = jnp.float32
BF16 = jnp.bfloat16

D_MODEL = 2048
GDN_QK_HEADS = 8
GDN_V_HEADS = 16
GDN_HEAD_DIM = 128
GDN_CONV = 4
GDN_CHUNK = 64
SWA_Q_HEADS = 32
SWA_KV_HEADS = 4
SWA_HEAD_DIM = 64
SWA_WINDOW = 128
REL_BUCKETS = 32
REL_MAX_DIST = 128
N_EXPERTS = 8
TOP_K = 2
NORM_EPS = 1e-6
GDN_KEY_DIM = GDN_QK_HEADS * GDN_HEAD_DIM
GDN_VAL_DIM = GDN_V_HEADS * GDN_HEAD_DIM
SWA_Q_DIM = SWA_Q_HEADS * SWA_HEAD_DIM
SWA_KV_DIM = SWA_KV_HEADS * SWA_HEAD_DIM

LANES = 128
VMEM_LIMIT = 56 * 1024 * 1024

COL_Q = 0
COL_K = COL_Q + GDN_KEY_DIM
COL_V = COL_K + GDN_KEY_DIM
COL_Z = COL_V + GDN_VAL_DIM
COL_SQ = COL_Z + GDN_VAL_DIM
COL_GA = COL_SQ + SWA_Q_DIM
COL_GB = COL_GA + D_MODEL
COL_SK = COL_GB + D_MODEL
COL_SV = COL_SK + SWA_KV_DIM
PROJ_COLS = COL_SV + SWA_KV_DIM
GDN_CONV_DIM = 2 * GDN_KEY_DIM + GDN_VAL_DIM

MASKED = -1e30
DMA_UNROLL = 8


def _cparams(n_axes):
    return pltpu.CompilerParams(dimension_semantics=("arbitrary",) * n_axes,
                                vmem_limit_bytes=VMEM_LIMIT)


def _rms(x, g):
    return x * lax.rsqrt(jnp.mean(x * x, axis=-1, keepdims=True) + NORM_EPS) * g


def _silu(x):
    return x * jax.nn.sigmoid(x)


def _dot(a, b):
    return jnp.dot(a, b, preferred_element_type=F32)


def _dot_nt(a, b):
    return lax.dot_general(a, b, (((1,), (1,)), ((), ())), preferred_element_type=F32)


def _dot_tn(a, b):
    return lax.dot_general(a, b, (((0,), (0,)), ((), ())), preferred_element_type=F32)


def _inproj_body(x_ref, g_ref, w_ref, wba_ref, o_ref, oba_ref, h_ref):
    @pl.when(pl.program_id(1) == 0)
    def _():
        h = _rms(x_ref[...], g_ref[...]).astype(BF16)
        h_ref[...] = h
        oba_ref[...] = _dot(h, wba_ref[...])

    o_ref[...] = _dot(h_ref[...], w_ref[...]).astype(o_ref.dtype)


def in_proj(x2, g, w_main, w_ba, tm, tn):
    T, D = x2.shape
    N = w_main.shape[1]
    return pl.pallas_call(
        _inproj_body,
        grid=(T // tm, N // tn),
        in_specs=[pl.BlockSpec((tm, D), lambda i, j: (i, 0)),
                  pl.BlockSpec((1, D), lambda i, j: (0, 0)),
                  pl.BlockSpec((D, tn), lambda i, j: (0, j)),
                  pl.BlockSpec((D, LANES), lambda i, j: (0, 0))],
        out_specs=[pl.BlockSpec((tm, tn), lambda i, j: (i, j)),
                   pl.BlockSpec((tm, LANES), lambda i, j: (i, 0))],
        out_shape=[jax.ShapeDtypeStruct((T, N), BF16), jax.ShapeDtypeStruct((T, LANES), F32)],
        scratch_shapes=[pltpu.VMEM((tm, D), BF16)],
        compiler_params=_cparams(2),
        name="in_proj",
    )(x2, g, w_main, w_ba)


def _prep_body(x_ref, halo_ref, cw_ref, ba_ref, alog_ref, dtb_ref, q_ref, k_ref, v_ref, gb_ref,
               *, ts, seq, n_qk, n_v, chunk):
    first = (pl.program_id(0) * ts) % seq == 0
    for c in range(2 * n_qk + n_v):
        cs = slice(c * LANES, (c + 1) * LANES)
        cur = x_ref[:, cs].astype(F32)
        hal = halo_ref[8:16, cs].astype(F32)
        hal = jnp.where(first, 0.0, hal)
        ext = jnp.concatenate([hal, cur], axis=0)
        y = ext[5:5 + ts] * cw_ref[0:1, cs]
        y = y + ext[6:6 + ts] * cw_ref[1:2, cs]
        y = y + ext[7:7 + ts] * cw_ref[2:3, cs]
        y = y + cur * cw_ref[3:4, cs]
        y = _silu(y)
        if c < 2 * n_qk:
            y = y * lax.rsqrt(jnp.sum(y * y, axis=-1, keepdims=True) + 1e-6)
        if c < n_qk:
            q_ref[:, cs] = (y * (GDN_HEAD_DIM ** -0.5)).astype(q_ref.dtype)
        elif c < 2 * n_qk:
            k_ref[:, (c - n_qk) * LANES:(c - n_qk + 1) * LANES] = y.astype(k_ref.dtype)
        else:
            v_ref[:, (c - 2 * n_qk) * LANES:(c - 2 * n_qk + 1) * LANES] = y.astype(v_ref.dtype)

    ba = ba_ref[...]
    lane = lax.broadcasted_iota(jnp.int32, ba.shape, 1)
    pos = lax.broadcasted_iota(jnp.int32, ba.shape, 0) % chunk
    beta = jax.nn.sigmoid(ba)
    t = ba + dtb_ref[...]
    softplus = jnp.maximum(t, 0.0) + jnp.log1p(jnp.exp(-jnp.abs(t)))
    gc = -jnp.exp(alog_ref[...]) * softplus
    s = 1
    while s < chunk:
        gc = gc + jnp.where(pos >= s, pltpu.roll(gc, s, 0), 0.0)
        s *= 2
    gb_ref[...] = jnp.where(lane < n_v, beta, jnp.where(lane < 2 * n_v, gc, 0.0))


def gdn_prep(proj, ba, conv_w, alog_pad, dtb_pad, seq, ts):
    T = proj.shape[0]
    n_qk, n_v = GDN_QK_HEADS, GDN_V_HEADS
    body = functools.partial(_prep_body, ts=ts, seq=seq, n_qk=n_qk, n_v=n_v, chunk=GDN_CHUNK)
    return pl.pallas_call(
        body,
        grid=(T // ts,),
        in_specs=[pl.BlockSpec((ts, GDN_CONV_DIM), lambda i: (i, 0)),
                  pl.BlockSpec((16, GDN_CONV_DIM), lambda i: (jnp.maximum(i * (ts // 16) - 1, 0), 0)),
                  pl.BlockSpec((GDN_CONV, GDN_CONV_DIM), lambda i: (0, 0)),
                  pl.BlockSpec((ts, LANES), lambda i: (i, 0)),
                  pl.BlockSpec((1, LANES), lambda i: (0, 0)),
                  pl.BlockSpec((1, LANES), lambda i: (0, 0))],
        out_specs=[pl.BlockSpec((ts, GDN_KEY_DIM), lambda i: (i, 0)),
                   pl.BlockSpec((ts, GDN_KEY_DIM), lambda i: (i, 0)),
                   pl.BlockSpec((ts, GDN_VAL_DIM), lambda i: (i, 0)),
                   pl.BlockSpec((ts, LANES), lambda i: (i, 0))],
        out_shape=[jax.ShapeDtypeStruct((T, GDN_KEY_DIM), BF16),
                   jax.ShapeDtypeStruct((T, GDN_KEY_DIM), BF16),
                   jax.ShapeDtypeStruct((T, GDN_VAL_DIM), BF16),
                   jax.ShapeDtypeStruct((T, LANES), F32)],
        compiler_params=_cparams(1),
        name="gdn_prep",
    )(proj, proj, conv_w, ba, alog_pad, dtb_pad)


def _gdn_body(q_ref, k_ref, v_ref, z_ref, gb_ref, gcr_ref, gn_ref, o_ref,
              u_s, wq_s, qkd_s, gl_s, st_s, *, n_chunks, n_qk, n_v):
    C = GDN_CHUNK
    hd = GDN_HEAD_DIM
    rep = n_v // n_qk

    @pl.when(pl.program_id(1) == 0)
    def _():
        st_s[...] = jnp.zeros_like(st_s)

    row = lax.broadcasted_iota(jnp.int32, (C, C), 0)
    col = lax.broadcasted_iota(jnp.int32, (C, C), 1)
    causal = row >= col
    strict = row > col

    hs = range(n_qk)
    hvs = range(n_v)

    def phase1(c, carry):
        rows = pl.ds(pl.multiple_of(c * C, C), C)
        rows2 = pl.ds(pl.multiple_of(c * 2 * C, 2 * C), C)
        rows2b = pl.ds(pl.multiple_of(c * 2 * C, 2 * C) + C, C)
        gbc = gb_ref[rows, :]
        q = [q_ref[rows, h * hd:(h + 1) * hd] for h in hs]
        k = [k_ref[rows, h * hd:(h + 1) * hd] for h in hs]
        kk = [_dot_nt(k[h], k[h]) for h in hs]
        qk = [_dot_nt(q[h], k[h]) for h in hs]
        bcol = [gbc[:, hv:hv + 1] for hv in hvs]
        gcol = [gbc[:, n_v + hv:n_v + hv + 1] for hv in hvs]
        glast = [gbc[C - 1:C, n_v + hv:n_v + hv + 1] for hv in hvs]
        decay = [jnp.exp(jnp.where(causal, gcol[hv] - gcr_ref[0, hv, pl.ds(c, 1), :], -jnp.inf)) for hv in hvs]
        bk = [jnp.where(strict, kk[hv // rep] * bcol[hv] * decay[hv], 0.0) * -1.0 for hv in hvs]
        nn = list(bk)
        egc = None
        p = 1
        while 2 * p < C:
            bb = [x.astype(BF16) for x in bk]
            bk = [_dot(x, x) for x in bb]
            if egc is None:
                egc = [jnp.exp(g) for g in gcol]
                for hv in hvs:
                    qf = q[hv // rep].astype(F32)
                    kf = k[hv // rep].astype(F32)
                    wq_s[hv, rows2b, :] = (qf * egc[hv]).astype(BF16)
                    kd = kf * jnp.exp(glast[hv] - gcol[hv])
                    qkd_s[hv, c, 0:C, :] = (qk[hv // rep] * decay[hv]).astype(BF16)
                    qkd_s[hv, c, C:3 * C, :] = kd.T.astype(BF16)
                    gl_s[hv, pl.ds(c, 1), :] = jnp.broadcast_to(jnp.exp(glast[hv]), (1, LANES))
            nn = [n + b + _dot(n.astype(BF16), b.astype(BF16)) for n, b in zip(nn, bk)]
            p *= 2
        nb = [n.astype(BF16) for n in nn]
        vb = [v_ref[rows, hv * hd:(hv + 1) * hd].astype(F32) * bcol[hv] for hv in hvs]
        kbg = [k[hv // rep].astype(F32) * (bcol[hv] * egc[hv]) for hv in hvs]
        uu = [_dot(nb[hv], vb[hv].astype(BF16)) for hv in hvs]
        ww = [_dot(nb[hv], kbg[hv].astype(BF16)) for hv in hvs]
        for hv in hvs:
            u_s[hv, rows, :] = vb[hv] + uu[hv]
            wq_s[hv, rows2, :] = (kbg[hv] + ww[hv]).astype(BF16)
        return carry

    lax.fori_loop(0, n_chunks, phase1, 0)

    def phase2(c, carry):
        rows = pl.ds(pl.multiple_of(c * C, C), C)
        rows_wq = pl.ds(pl.multiple_of(c * 2 * C, 2 * C), 2 * C)
        sb = [st_s[hv].astype(BF16) for hv in hvs]
        ws = [_dot(wq_s[hv, rows_wq, :], sb[hv]) for hv in hvs]
        vnb = [(u_s[hv, rows, :] - ws[hv][0:C]).astype(BF16) for hv in hvs]
        ov = [_dot(qkd_s[hv, c], vnb[hv]) for hv in hvs]
        for hv in hvs:
            st_s[hv] = st_s[hv] * gl_s[hv, pl.ds(c, 1), :] + ov[hv][C:3 * C]
        for hv in hvs:
            o = ws[hv][C:2 * C] + ov[hv][0:C]
            o = o * lax.rsqrt(jnp.mean(o * o, axis=-1, keepdims=True) + NORM_EPS) * gn_ref[...]
            o = o * _silu(z_ref[rows, hv * hd:(hv + 1) * hd].astype(F32))
            o_ref[rows, hv * hd:(hv + 1) * hd] = o.astype(o_ref.dtype)
        return carry

    lax.fori_loop(0, n_chunks, phase2, 0)


def gdn_core(qn, kn, vv, proj, gb, gcr, gnorm, batch, seq, sb):
    T = qn.shape[0]
    n_qk, n_v, hd, C = GDN_QK_HEADS, GDN_V_HEADS, GDN_HEAD_DIM, GDN_CHUNK
    nc = sb // C
    nsb = seq // sb
    zc = COL_Z // (n_v * hd)
    body = functools.partial(_gdn_body, n_chunks=nc, n_qk=n_qk, n_v=n_v)
    blk = lambda b, s: (b * nsb + s, 0)
    return pl.pallas_call(
        body,
        grid=(batch, nsb),
        in_specs=[pl.BlockSpec((sb, n_qk * hd), blk),
                  pl.BlockSpec((sb, n_qk * hd), blk),
                  pl.BlockSpec((sb, n_v * hd), blk),
                  pl.BlockSpec((sb, n_v * hd), lambda b, s: (b * nsb + s, zc)),
                  pl.BlockSpec((sb, LANES), blk),
                  pl.BlockSpec((1, n_v, nc, C), lambda b, s: (b, 0, s, 0)),
                  pl.BlockSpec((1, hd), lambda b, s: (0, 0))],
        out_specs=pl.BlockSpec((sb, n_v * hd), blk),
        out_shape=jax.ShapeDtypeStruct((T, n_v * hd), BF16),
        scratch_shapes=[pltpu.VMEM((n_v, sb, hd), F32),
                        pltpu.VMEM((n_v, 2 * sb, hd), BF16),
                        pltpu.VMEM((n_v, nc, 3 * C, C), BF16),
                        pltpu.VMEM((n_v, nc, LANES), F32),
                        pltpu.VMEM((n_v, hd, hd), F32)],
        compiler_params=_cparams(2),
        name="gdn_core",
    )(qn, kn, vv, proj, gb, gcr, gnorm)


def _swa_body(q_ref, kp_ref, kc_ref, vp_ref, vc_ref, bias_ref, sink_ref, o_ref):
    W, dh = SWA_WINDOW, SWA_HEAD_DIM
    pairs = (SWA_Q_HEADS // SWA_KV_HEADS) // 2
    kvs = range(SWA_KV_HEADS)
    zero = jnp.zeros((2 * W, dh), BF16)
    one = jnp.ones((2 * W, dh), BF16)
    low = lax.broadcasted_iota(jnp.int32, (1, LANES), 1) < dh

    logits = []
    for hk in kvs:
        ks = slice(hk * dh, (hk + 1) * dh)
        kb = jnp.concatenate([kp_ref[:, ks], kc_ref[:, ks]], axis=0)
        krhs = jnp.concatenate([jnp.concatenate([kb, zero], axis=1),
                                jnp.concatenate([zero, kb], axis=1)], axis=0)
        qs = jnp.concatenate([q_ref[:, (hk * pairs + p) * LANES:(hk * pairs + p + 1) * LANES]
                              for p in range(pairs)], axis=0)
        qs = qs * jnp.asarray(dh ** -0.5, BF16)
        logits.append(_dot_nt(qs, krhs) + bias_ref[0, hk])

    outs, sinks = [], []
    for hk in kvs:
        ks = slice(hk * dh, (hk + 1) * dh)
        vb = jnp.concatenate([vp_ref[:, ks], vc_ref[:, ks]], axis=0)
        vrhs = jnp.concatenate([jnp.concatenate([vb, zero, one, zero], axis=1),
                                jnp.concatenate([zero, vb, zero, one], axis=1)], axis=0)
        ps, ms = [], []
        for c in range(2):
            lg = logits[hk][:, c * 2 * W:(c + 1) * 2 * W]
            sk = sink_ref[hk, c]
            m = jnp.max(jnp.maximum(jnp.maximum(lg[:, :W], lg[:, W:]), sk), axis=-1, keepdims=True)
            ps.append(jnp.exp(lg - m).astype(BF16))
            ms.append(sk - m)
        outs.append(_dot(jnp.concatenate(ps, axis=1), vrhs))
        sinks.append(jnp.exp(jnp.where(low, ms[0], ms[1])))

    for hk in kvs:
        out = outs[hk][:, :LANES] / (outs[hk][:, LANES:] + sinks[hk])
        for p in range(pairs):
            o_ref[:, (hk * pairs + p) * LANES:(hk * pairs + p + 1) * LANES] = (
                out[p * W:(p + 1) * W].astype(o_ref.dtype))


def swa_core(proj, bias_tbl, sink_tbl, batch, seq):
    T = proj.shape[0]
    W = SWA_WINDOW
    nb = seq // W
    qc, kc, vc = COL_SQ // SWA_Q_DIM, COL_SK // SWA_KV_DIM, COL_SV // SWA_KV_DIM
    cur = lambda b, n: b * nb + n
    prev = lambda b, n: jnp.maximum(b * nb + n - 1, 0)
    return pl.pallas_call(
        _swa_body,
        grid=(batch, nb),
        in_specs=[pl.BlockSpec((W, SWA_Q_DIM), lambda b, n: (cur(b, n), qc)),
                  pl.BlockSpec((W, SWA_KV_DIM), lambda b, n: (prev(b, n), kc)),
                  pl.BlockSpec((W, SWA_KV_DIM), lambda b, n: (cur(b, n), kc)),
                  pl.BlockSpec((W, SWA_KV_DIM), lambda b, n: (prev(b, n), vc)),
                  pl.BlockSpec((W, SWA_KV_DIM), lambda b, n: (cur(b, n), vc)),
                  pl.BlockSpec((1,) + bias_tbl.shape[1:], lambda b, n: (jnp.minimum(n, 1), 0, 0, 0)),
                  pl.BlockSpec(sink_tbl.shape, lambda b, n: (0, 0, 0, 0))],
        out_specs=pl.BlockSpec((W, SWA_Q_DIM), lambda b, n: (cur(b, n), 0)),
        out_shape=jax.ShapeDtypeStruct((T, SWA_Q_DIM), BF16),
        compiler_params=_cparams(2),
        name="swa_core",
    )(proj, proj, proj, proj, proj, bias_tbl, sink_tbl)


def _t5_bucket(dist):
    max_exact = REL_BUCKETS // 2
    large = max_exact + (jnp.log(jnp.maximum(dist, 1).astype(F32) / max_exact)
                         / math.log(REL_MAX_DIST / max_exact) * (REL_BUCKETS - max_exact)).astype(jnp.int32)
    large = jnp.minimum(large, REL_BUCKETS - 1)
    return jnp.where(dist < max_exact, dist, large)


def swa_tables(rel_bias, sinks):
    W, hq, hkv = SWA_WINDOW, SWA_Q_HEADS, SWA_KV_HEADS
    pairs = (hq // hkv) // 2
    by_dist = rel_bias.astype(F32)[_t5_bucket(jnp.arange(W))]
    period = 3 * W
    vec = jnp.concatenate([jnp.full((1, hq), MASKED, F32), by_dist[::-1],
                           jnp.full((period - W - 1, hq), MASKED, F32)], axis=0).T
    tbl = jnp.tile(vec, (1, W))[:, :W * (period - 1)].reshape(hq, W, period - 1)[:, :, :2 * W]
    first = jnp.where(jnp.arange(2 * W)[None, None, :] < W, MASKED, tbl)
    tbl = jnp.stack([first, tbl])
    tbl = tbl.reshape(2, hkv, pairs, 2, W, 2 * W).transpose(0, 1, 2, 4, 3, 5)
    tbl = tbl.reshape(2, hkv, pairs * W, 4 * W)
    sk = sinks.astype(F32).reshape(hkv, pairs, 2).transpose(0, 2, 1)
    sk = jnp.broadcast_to(sk[:, :, :, None, None], (hkv, 2, pairs, W, LANES)).reshape(hkv, 2, pairs * W, LANES)
    return tbl, sk


def _merge_body(oa_ref, ob_ref, wa_ref, wb_ref, ga_ref, gb_ref, y_ref):
    a = _dot(oa_ref[...], wa_ref[...])
    b = _dot(ob_ref[...], wb_ref[...])
    y = jax.nn.sigmoid(ga_ref[...].astype(F32)) * a + jax.nn.sigmoid(gb_ref[...].astype(F32)) * b
    y_ref[...] = y.astype(y_ref.dtype)


def merge_proj(o_a, o_b, w_a, w_b, proj, tm, tn):
    T, K = o_a.shape
    N = w_a.shape[1]
    ga0, gb0 = COL_GA // tn, COL_GB // tn
    return pl.pallas_call(
        _merge_body,
        grid=(T // tm, N // tn),
        in_specs=[pl.BlockSpec((tm, K), lambda i, j: (i, 0)),
                  pl.BlockSpec((tm, K), lambda i, j: (i, 0)),
                  pl.BlockSpec((K, tn), lambda i, j: (0, j)),
                  pl.BlockSpec((K, tn), lambda i, j: (0, j)),
                  pl.BlockSpec((tm, tn), lambda i, j: (i, ga0 + j)),
                  pl.BlockSpec((tm, tn), lambda i, j: (i, gb0 + j))],
        out_specs=pl.BlockSpec((tm, tn), lambda i, j: (i, j)),
        out_shape=jax.ShapeDtypeStruct((T, N), BF16),
        compiler_params=_cparams(2),
        name="merge_proj",
    )(o_a, o_b, w_a, w_b, proj, proj)


def _resid_body(y_ref, w_ref, x_ref, o_ref):
    o_ref[...] = x_ref[...] + _dot(y_ref[...], w_ref[...])


def out_proj(y, w, x2, tm, tn):
    T, K = y.shape
    N = w.shape[1]
    return pl.pallas_call(
        _resid_body,
        grid=(T // tm, N // tn),
        in_specs=[pl.BlockSpec((tm, K), lambda i, j: (i, 0)),
                  pl.BlockSpec((K, tn), lambda i, j: (0, j)),
                  pl.BlockSpec((tm, tn), lambda i, j: (i, j))],
        out_specs=pl.BlockSpec((tm, tn), lambda i, j: (i, j)),
        out_shape=jax.ShapeDtypeStruct((T, N), F32),
        compiler_params=_cparams(2),
        name="out_proj",
    )(y, w, x2)


def _ffn_body(x_ref, g_ref, wg_ref, wu_ref, wd_ref, o_ref, h_ref):
    @pl.when(pl.program_id(1) == 0)
    def _():
        x = x_ref[...]
        h_ref[...] = _rms(x, g_ref[...]).astype(BF16)
        o_ref[...] = x

    h = h_ref[...]
    mid = (_silu(_dot(h, wg_ref[...])) * _dot(h, wu_ref[...])).astype(BF16)
    o_ref[...] += _dot(mid, wd_ref[...])


def dense_ffn(x2, g, w_gate, w_up, w_down, tm, tf):
    T, D = x2.shape
    F = w_gate.shape[1]
    return pl.pallas_call(
        _ffn_body,
        grid=(T // tm, F // tf),
        in_specs=[pl.BlockSpec((tm, D), lambda i, f: (i, 0)),
                  pl.BlockSpec((1, D), lambda i, f: (0, 0)),
                  pl.BlockSpec((D, tf), lambda i, f: (0, f)),
                  pl.BlockSpec((D, tf), lambda i, f: (0, f)),
                  pl.BlockSpec((tf, D), lambda i, f: (f, 0))],
        out_specs=pl.BlockSpec((tm, D), lambda i, f: (i, 0)),
        out_shape=jax.ShapeDtypeStruct((T, D), F32),
        scratch_shapes=[pltpu.VMEM((tm, D), BF16)],
        compiler_params=_cparams(2),
        name="dense_ffn",
    )(x2, g, w_gate, w_up, w_down)


def _router_body(x_ref, g_ref, rwt_ref, e_ref, w_ref):
    h = _rms(x_ref[...], g_ref[...]).astype(BF16)
    lt = _dot_nt(rwt_ref[...], h)
    idx = lax.broadcasted_iota(jnp.int32, lt.shape, 0)
    m1 = jnp.max(lt, axis=0, keepdims=True)
    i1 = jnp.min(jnp.where(lt == m1, idx, N_EXPERTS), axis=0, keepdims=True)
    lt2 = jnp.where(idx == i1, -jnp.inf, lt)
    m2 = jnp.max(lt2, axis=0, keepdims=True)
    i2 = jnp.min(jnp.where(lt2 == m2, idx, N_EXPERTS), axis=0, keepdims=True)
    e2 = jnp.exp(m2 - m1)
    den = 1.0 + e2
    e_ref[...] = jnp.concatenate([i1, i2], axis=0)
    w_ref[...] = jnp.concatenate([1.0 / den, e2 / den], axis=0)


def moe_router(x2, g, rw_t, tm):
    T, D = x2.shape
    return pl.pallas_call(
        _router_body,
        grid=(T // tm,),
        in_specs=[pl.BlockSpec((tm, D), lambda i: (i, 0)),
                  pl.BlockSpec((1, D), lambda i: (0, 0)),
                  pl.BlockSpec((N_EXPERTS, D), lambda i: (0, 0))],
        out_specs=[pl.BlockSpec((TOP_K, tm), lambda i: (0, i)),
                   pl.BlockSpec((TOP_K, tm), lambda i: (0, i))],
        out_shape=[jax.ShapeDtypeStruct((TOP_K, T), jnp.int32),
                   jax.ShapeDtypeStruct((TOP_K, T), F32)],
        compiler_params=_cparams(1),
        name="moe_router",
    )(x2, g, rw_t)


def _moe_body(be_ref, nv_ref, nu_ref, tok_ref, x_hbm, g_ref, wg_ref, wu_ref, wd_ref, o_ref,
              xbuf, h_ref, sem, *, tm):
    i = pl.program_id(0)
    f = pl.program_id(1)
    n_used = nu_ref[0]
    hm = tm // 2

    def row_copy(idx, r):
        return pltpu.make_async_copy(x_hbm.at[pl.ds(idx, 1), :], xbuf.at[pl.ds(r, 1), :], sem.at[0])

    def gather(blk):
        def issue(r8, c):
            for u in range(DMA_UNROLL):
                r = r8 * DMA_UNROLL + u
                row_copy(tok_ref[blk * tm + r], r).start()
            return c
        n_rows = jnp.where(nv_ref[blk] > hm, tm, hm)
        lax.fori_loop(0, n_rows // DMA_UNROLL, issue, 0)

    def wait_rows(n_rows):
        for r in range(n_rows):
            row_copy(0, r).wait()

    def compute(rows):
        h = h_ref[0:rows, :]
        mid = (_silu(_dot(h, wg_ref[...])) * _dot(h, wu_ref[...])).astype(BF16)
        o_ref[0:rows, :] += _dot(mid, wd_ref[...])

    active = i < n_used
    full = nv_ref[i] > hm

    @pl.when((f == 0) & (i == 0))
    def _():
        gather(0)

    @pl.when((f == 0) & active & full)
    def _():
        wait_rows(tm)
        h_ref[...] = _rms(xbuf[...], g_ref[...]).astype(BF16)

    @pl.when((f == 0) & active & jnp.logical_not(full))
    def _():
        wait_rows(hm)
        h_ref[0:hm, :] = _rms(xbuf[0:hm, :], g_ref[...]).astype(BF16)

    @pl.when((f == 1) & (i + 1 < n_used))
    def _():
        gather(i + 1)

    @pl.when(f == 0)
    def _():
        o_ref[...] = jnp.zeros_like(o_ref)

    @pl.when(active & full)
    def _():
        compute(tm)

    @pl.when(active & jnp.logical_not(full))
    def _():
        compute(hm)


def moe_experts(x2, g, w_gate, w_up, w_down, block_e, n_valid, n_used, row_tok, tm, tf):
    T, D = x2.shape
    E, _, F = w_gate.shape
    n_rows = row_tok.shape[0]
    nf = F // tf
    assert nf >= 2

    def fidx(i, f, nu):
        return jnp.where(i < nu[0], f, nf - 1)

    body = functools.partial(_moe_body, tm=tm)
    return pl.pallas_call(
        body,
        grid_spec=pltpu.PrefetchScalarGridSpec(
            num_scalar_prefetch=4,
            grid=(n_rows // tm, nf),
            in_specs=[pl.BlockSpec(memory_space=pl.ANY),
                      pl.BlockSpec((1, D), lambda i, f, be, nv, nu, tok: (0, 0)),
                      pl.BlockSpec((None, D, tf), lambda i, f, be, nv, nu, tok: (be[i], 0, fidx(i, f, nu))),
                      pl.BlockSpec((None, D, tf), lambda i, f, be, nv, nu, tok: (be[i], 0, fidx(i, f, nu))),
                      pl.BlockSpec((None, tf, D), lambda i, f, be, nv, nu, tok: (be[i], fidx(i, f, nu), 0))],
            out_specs=pl.BlockSpec((tm, D), lambda i, f, be, nv, nu, tok: (i, 0)),
            scratch_shapes=[pltpu.VMEM((tm, D), F32),
                            pltpu.VMEM((tm, D), BF16),
                            pltpu.SemaphoreType.DMA((1,))]),
        out_shape=jax.ShapeDtypeStruct((n_rows, D), F32),
        compiler_params=_cparams(2),
        name="moe_experts",
    )(block_e, n_valid, n_used, row_tok, x2, g, w_gate, w_up, w_down)


def _combine_body(dest_ref, yg_hbm, x_ref, tw_ref, g_ref, o_ref, ybuf, sem, *, tt, n_steps, final_norm):
    i = pl.program_id(0)

    def gather(blk, slot):
        def issue(r8, c):
            for u in range(DMA_UNROLL):
                r = r8 * DMA_UNROLL + u
                for k in range(TOP_K):
                    pltpu.make_async_copy(yg_hbm.at[pl.ds(dest_ref[(blk * tt + r) * TOP_K + k], 1), :],
                                          ybuf.at[slot, k, pl.ds(r, 1), :], sem.at[slot]).start()
            return c
        lax.fori_loop(0, tt // DMA_UNROLL, issue, 0)

    @pl.when(i == 0)
    def _():
        gather(0, 0)

    slot = i % 2

    @pl.when(i + 1 < n_steps)
    def _():
        gather(i + 1, 1 - slot)

    for r in range(tt):
        for k in range(TOP_K):
            pltpu.make_async_copy(yg_hbm.at[pl.ds(0, 1), :], ybuf.at[slot, k, pl.ds(r, 1), :],
                                  sem.at[slot]).wait()

    tw = tw_ref[...]
    y = x_ref[...] + ybuf[slot, 0] * tw[:, 0:1] + ybuf[slot, 1] * tw[:, 1:2]
    if final_norm:
        y = _rms(y, g_ref[...])
    o_ref[...] = y


def moe_combine(yg, x2, tw_col, dest, g_final, tt, final_norm):
    T, D = x2.shape
    n_steps = T // tt
    body = functools.partial(_combine_body, tt=tt, n_steps=n_steps, final_norm=final_norm)
    return pl.pallas_call(
        body,
        grid_spec=pltpu.PrefetchScalarGridSpec(
            num_scalar_prefetch=1,
            grid=(n_steps,),
            in_specs=[pl.BlockSpec(memory_space=pl.ANY),
                      pl.BlockSpec((tt, D), lambda i, d: (i, 0)),
                      pl.BlockSpec((tt, LANES), lambda i, d: (i, 0)),
                      pl.BlockSpec((1, D), lambda i, d: (0, 0))],
            out_specs=pl.BlockSpec((tt, D), lambda i, d: (i, 0)),
            scratch_shapes=[pltpu.VMEM((2, TOP_K, tt, D), F32),
                            pltpu.SemaphoreType.DMA((2,))]),
        out_shape=jax.ShapeDtypeStruct((T, D), F32),
        compiler_params=_cparams(1),
        name="moe_combine",
    )(dest, yg, x2, tw_col, g_final)


def moe_plan(top_e, tm):
    T = top_e.shape[1]
    TK = T * TOP_K
    flat_e = top_e.T.reshape(TK)
    onehot = (flat_e[:, None] == jnp.arange(N_EXPERTS, dtype=jnp.int32)[None, :]).astype(jnp.int32)
    cum = jnp.cumsum(onehot, axis=0)
    counts = cum[-1]
    rank = jnp.sum(onehot * (cum - 1), axis=1)
    padded = (counts + tm - 1) // tm * tm
    pad_end = jnp.cumsum(padded)
    pad_start = pad_end - padded
    dest = jnp.sum(onehot * pad_start[None, :], axis=1) + rank
    n_rows = TK + N_EXPERTS * tm
    n_blocks = n_rows // tm
    row_tok = jnp.zeros((n_rows,), jnp.int32).at[dest].set(jnp.arange(TK, dtype=jnp.int32) // TOP_K)
    blk = jnp.arange(n_blocks, dtype=jnp.int32)
    blk_onehot = ((blk[:, None] * tm >= pad_start[None, :]) & (blk[:, None] * tm < pad_end[None, :])).astype(jnp.int32)
    n_used = (pad_end[-1:] // tm).astype(jnp.int32)
    experts = jnp.arange(N_EXPERTS, dtype=jnp.int32)
    last_e = jnp.max(jnp.where(padded > 0, experts, 0))
    block_e = jnp.where(blk < n_used[0], jnp.sum(blk_onehot * experts[None, :], axis=1), last_e)
    n_valid = jnp.clip(jnp.sum(blk_onehot * (counts + pad_start)[None, :], axis=1) - blk * tm, 0, tm)
    return dest.astype(jnp.int32), row_tok, block_e.astype(jnp.int32), n_valid.astype(jnp.int32), n_used


def moe_block(x2, g, router_w, w_gate, w_up, w_down, g_final, final_norm, tm_r, tm_e, tf, tt):
    top_e, top_w = moe_router(x2, g, router_w.T.astype(BF16), tm_r)
    dest, row_tok, block_e, n_valid, n_used = moe_plan(top_e, tm_e)
    yg = moe_experts(x2, g, w_gate, w_up, w_down, block_e, n_valid, n_used, row_tok, tm_e, tf)
    tw_col = jnp.pad(top_w.T, ((0, 0), (0, LANES - TOP_K)))
    return moe_combine(yg, x2, tw_col, dest, g_final, tt, final_norm)


def _norm_body(x_ref, g_ref, o_ref):
    o_ref[...] = _rms(x_ref[...], g_ref[...])


def final_norm(x2, g, tm):
    T, D = x2.shape
    return pl.pallas_call(
        _norm_body,
        grid=(T // tm,),
        in_specs=[pl.BlockSpec((tm, D), lambda i: (i, 0)), pl.BlockSpec((1, D), lambda i: (0, 0))],
        out_specs=pl.BlockSpec((tm, D), lambda i: (i, 0)),
        out_shape=jax.ShapeDtypeStruct((T, D), F32),
        compiler_params=_cparams(1),
        name="final_norm",
    )(x2, g)


def _split_w_in(w):
    sizes = (GDN_KEY_DIM, GDN_KEY_DIM, GDN_VAL_DIM, GDN_VAL_DIM, GDN_V_HEADS, GDN_V_HEADS,
             SWA_Q_DIM, SWA_KV_DIM, SWA_KV_DIM, D_MODEL, D_MODEL)
    offs = [0]
    for s in sizes:
        offs.append(offs[-1] + s)
    q, k, v, z, b, a, sq, sk, sv, ga, gb = [w[:, offs[n]:offs[n + 1]] for n in range(len(sizes))]
    w_main = jnp.concatenate([q, k, v, z, sq, ga, gb, sk, sv], axis=1).astype(BF16)
    w_ba = jnp.pad(jnp.concatenate([b, a], axis=1), ((0, 0), (0, LANES - 2 * GDN_V_HEADS))).astype(BF16)
    return w_main, w_ba


def _pad_lanes(vec, offset):
    return jnp.pad(vec.astype(F32), (offset, LANES - offset - vec.shape[0])).reshape(1, LANES)


def mixer_layer(x2, batch, seq, ln_g, w_in, conv_w, a_log, dt_bias, gdn_norm_g, bias_tbl, sink_tbl,
                w_proj_gdn, w_proj_swa, w_out):
    T = x2.shape[0]
    w_main, w_ba = _split_w_in(w_in)
    proj, ba = in_proj(x2, ln_g.reshape(1, -1), w_main, w_ba, tm=1024, tn=1280)
    qn, kn, vv, gb = gdn_prep(proj, ba, conv_w, _pad_lanes(a_log, GDN_V_HEADS),
                              _pad_lanes(dt_bias, GDN_V_HEADS), seq, ts=256)
    n_chunks = seq // GDN_CHUNK
    gcr = gb[:, GDN_V_HEADS:2 * GDN_V_HEADS].reshape(batch, n_chunks, GDN_CHUNK, GDN_V_HEADS)
    gcr = gcr.transpose(0, 3, 1, 2)
    o_a = gdn_core(qn, kn, vv, proj, gb, gcr, gdn_norm_g.reshape(1, -1).astype(F32), batch, seq, sb=512)
    o_b = swa_core(proj, bias_tbl, sink_tbl, batch, seq)
    y = merge_proj(o_a, o_b, w_proj_gdn.astype(BF16), w_proj_swa.astype(BF16), proj, tm=1024, tn=512)
    return out_proj(y, w_out.astype(BF16), x2, tm=1024, tn=512)


def kernel(x, ln_mix_g, w_in, conv_w, a_log, dt_bias, gdn_norm_g, sinks, rel_bias, w_proj_gdn, w_proj_swa,
           w_out, ln_ffn_g, ffn_w_gate, ffn_w_up, ffn_w_down, router_w, moe_w_gate, moe_w_up, moe_w_down,
           ln_final_g):
    B, S, D = x.shape
    depth = w_in.shape[0]
    x2 = x.reshape(B * S, D)
    g_final = ln_final_g.reshape(1, D)
    normed = False
    for i in range(depth):
        bias_tbl, sink_tbl = swa_tables(rel_bias, sinks[i])
        x2 = mixer_layer(x2, B, S, ln_mix_g[i], w_in[i], conv_w[i], a_log[i], dt_bias[i], gdn_norm_g[i],
                         bias_tbl, sink_tbl, w_proj_gdn[i], w_proj_swa[i], w_out[i])
        g_ffn = ln_ffn_g[i].reshape(1, D)
        j = i // 2
        if i % 2 == 0:
            f = ffn_w_gate.shape[2]
            fpad = (-f) % 512
            wg = jnp.pad(ffn_w_gate[j], ((0, 0), (0, fpad))).astype(BF16)
            wu = jnp.pad(ffn_w_up[j], ((0, 0), (0, fpad))).astype(BF16)
            wd = jnp.pad(ffn_w_down[j], ((0, fpad), (0, 0))).astype(BF16)
            x2 = dense_ffn(x2, g_ffn, wg, wu, wd, tm=1024, tf=512)
        else:
            last = i == depth - 1
            x2 = moe_block(x2, g_ffn, router_w[j], moe_w_gate[j].astype(BF16), moe_w_up[j].astype(BF16),
                           moe_w_down[j].astype(BF16), g_final, last, tm_r=512, tm_e=1024, tf=512, tt=256)
            normed = last
    if not normed:
        x2 = final_norm(x2, g_final, tm=512)
    return x2.reshape(B, S, D)
```

```python
import functools
import math

import jax
import jax.numpy as jnp
from jax import lax
from jax.experimental import pallas as pl
from jax.experimental.pallas import tpu as pltpu

F32 = jnp.float32
BF16 = jnp.bfloat16

D_MODEL = 2048
GDN_QK_HEADS = 8
GDN_V_HEADS = 16
GDN_HEAD_DIM = 128
GDN_CONV = 4
GDN_CHUNK = 64
SWA_Q_HEADS = 32
SWA_KV_HEADS = 4
SWA_HEAD_DIM = 64
SWA_WINDOW = 128
REL_BUCKETS = 32
REL_MAX_DIST = 128
N_EXPERTS = 8
TOP_K = 2
NORM_EPS = 1e-6
GDN_KEY_DIM = GDN_QK_HEADS * GDN_HEAD_DIM
GDN_VAL_DIM = GDN_V_HEADS * GDN_HEAD_DIM
SWA_Q_DIM = SWA_Q_HEADS * SWA_HEAD_DIM
SWA_KV_DIM = SWA_KV_HEADS * SWA_HEAD_DIM

LANES = 128
VMEM_LIMIT = 56 * 1024 * 1024

COL_Q = 0
COL_K = COL_Q + GDN_KEY_DIM
COL_V = COL_K + GDN_KEY_DIM
COL_Z = COL_V + GDN_VAL_DIM
COL_SQ = COL_Z + GDN_VAL_DIM
COL_GA = COL_SQ + SWA_Q_DIM
COL_GB = COL_GA + D_MODEL
COL_SK = COL_GB + D_MODEL
COL_SV = COL_SK + SWA_KV_DIM
PROJ_COLS = COL_SV + SWA_KV_DIM
GDN_CONV_DIM = 2 * GDN_KEY_DIM + GDN_VAL_DIM

MASKED = -1e30
DMA_UNROLL = 8


def _cparams(n_axes):
    return pltpu.CompilerParams(dimension_semantics=("arbitrary",) * n_axes,
                                vmem_limit_bytes=VMEM_LIMIT)


def _rms(x, g):
    return x * lax.rsqrt(jnp.mean(x * x, axis=-1, keepdims=True) + NORM_EPS) * g


def _silu(x):
    return x * jax.nn.sigmoid(x)


def _dot(a, b):
    return jnp.dot(a, b, preferred_element_type=F32)


def _dot_nt(a, b):
    return lax.dot_general(a, b, (((1,), (1,)), ((), ())), preferred_element_type=F32)


def _dot_tn(a, b):
    return lax.dot_general(a, b, (((0,), (0,)), ((), ())), preferred_element_type=F32)


def _inproj_body(x_ref, g_ref, w_ref, wba_ref, o_ref, oba_ref, h_ref):
    @pl.when(pl.program_id(1) == 0)
    def _():
        h = _rms(x_ref[...], g_ref[...]).astype(BF16)
        h_ref[...] = h
        oba_ref[...] = _dot(h, wba_ref[...])

    o_ref[...] = _dot(h_ref[...], w_ref[...]).astype(o_ref.dtype)


def in_proj(x2, g, w_main, w_ba, tm, tn):
    T, D = x2.shape
    N = w_main.shape[1]
    return pl.pallas_call(
        _inproj_body,
        grid=(T // tm, N // tn),
        in_specs=[pl.BlockSpec((tm, D), lambda i, j: (i, 0)),
                  pl.BlockSpec((1, D), lambda i, j: (0, 0)),
                  pl.BlockSpec((D, tn), lambda i, j: (0, j)),
                  pl.BlockSpec((D, LANES), lambda i, j: (0, 0))],
        out_specs=[pl.BlockSpec((tm, tn), lambda i, j: (i, j)),
                   pl.BlockSpec((tm, LANES), lambda i, j: (i, 0))],
        out_shape=[jax.ShapeDtypeStruct((T, N), BF16), jax.ShapeDtypeStruct((T, LANES), F32)],
        scratch_shapes=[pltpu.VMEM((tm, D), BF16)],
        compiler_params=_cparams(2),
        name="in_proj",
    )(x2, g, w_main, w_ba)


def _prep_body(x_ref, halo_ref, cw_ref, ba_ref, alog_ref, dtb_ref, q_ref, k_ref, v_ref, gb_ref,
               *, ts, seq, n_qk, n_v, chunk):
    first = (pl.program_id(0) * ts) % seq == 0
    for c in range(2 * n_qk + n_v):
        cs = slice(c * LANES, (c + 1) * LANES)
        cur = x_ref[:, cs].astype(F32)
        hal = halo_ref[8:16, cs].astype(F32)
        hal = jnp.where(first, 0.0, hal)
        ext = jnp.concatenate([hal, cur], axis=0)
        y = ext[5:5 + ts] * cw_ref[0:1, cs]
        y = y + ext[6:6 + ts] * cw_ref[1:2, cs]
        y = y + ext[7:7 + ts] * cw_ref[2:3, cs]
        y = y + cur * cw_ref[3:4, cs]
        y = _silu(y)
        if c < 2 * n_qk:
            y = y * lax.rsqrt(jnp.sum(y * y, axis=-1, keepdims=True) + 1e-6)
        if c < n_qk:
            q_ref[:, cs] = (y * (GDN_HEAD_DIM ** -0.5)).astype(q_ref.dtype)
        elif c < 2 * n_qk:
            k_ref[:, (c - n_qk) * LANES:(c - n_qk + 1) * LANES] = y.astype(k_ref.dtype)
        else:
            v_ref[:, (c - 2 * n_qk) * LANES:(c - 2 * n_qk + 1) * LANES] = y.astype(v_ref.dtype)

    ba = ba_ref[...]
    lane = lax.broadcasted_iota(jnp.int32, ba.shape, 1)
    pos = lax.broadcasted_iota(jnp.int32, ba.shape, 0) % chunk
    beta = jax.nn.sigmoid(ba)
    t = ba + dtb_ref[...]
    softplus = jnp.maximum(t, 0.0) + jnp.log1p(jnp.exp(-jnp.abs(t)))
    gc = -jnp.exp(alog_ref[...]) * softplus
    s = 1
    while s < chunk:
        gc = gc + jnp.where(pos >= s, pltpu.roll(gc, s, 0), 0.0)
        s *= 2
    gb_ref[...] = jnp.where(lane < n_v, beta, jnp.where(lane < 2 * n_v, gc, 0.0))


def gdn_prep(proj, ba, conv_w, alog_pad, dtb_pad, seq, ts):
    T = proj.shape[0]
    n_qk, n_v = GDN_QK_HEADS, GDN_V_HEADS
    body = functools.partial(_prep_body, ts=ts, seq=seq, n_qk=n_qk, n_v=n_v, chunk=GDN_CHUNK)
    return pl.pallas_call(
        body,
        grid=(T // ts,),
        in_specs=[pl.BlockSpec((ts, GDN_CONV_DIM), lambda i: (i, 0)),
                  pl.BlockSpec((16, GDN_CONV_DIM), lambda i: (jnp.maximum(i * (ts // 16) - 1, 0), 0)),
                  pl.BlockSpec((GDN_CONV, GDN_CONV_DIM), lambda i: (0, 0)),
                  pl.BlockSpec((ts, LANES), lambda i: (i, 0)),
                  pl.BlockSpec((1, LANES), lambda i: (0, 0)),
                  pl.BlockSpec((1, LANES), lambda i: (0, 0))],
        out_specs=[pl.BlockSpec((ts, GDN_KEY_DIM), lambda i: (i, 0)),
                   pl.BlockSpec((ts, GDN_KEY_DIM), lambda i: (i, 0)),
                   pl.BlockSpec((ts, GDN_VAL_DIM), lambda i: (i, 0)),
                   pl.BlockSpec((ts, LANES), lambda i: (i, 0))],
        out_shape=[jax.ShapeDtypeStruct((T, GDN_KEY_DIM), BF16),
                   jax.ShapeDtypeStruct((T, GDN_KEY_DIM), BF16),
                   jax.ShapeDtypeStruct((T, GDN_VAL_DIM), BF16),
                   jax.ShapeDtypeStruct((T, LANES), F32)],
        compiler_params=_cparams(1),
        name="gdn_prep",
    )(proj, proj, conv_w, ba, alog_pad, dtb_pad)


def _gdn_body(q_ref, k_ref, v_ref, z_ref, gb_ref, gcr_ref, gn_ref, o_ref,
              u_s, wq_s, qkd_s, gl_s, st_s, *, n_chunks, n_qk, n_v):
    C = GDN_CHUNK
    hd = GDN_HEAD_DIM
    rep = n_v // n_qk

    @pl.when(pl.program_id(1) == 0)
    def _():
        st_s[...] = jnp.zeros_like(st_s)

    row = lax.broadcasted_iota(jnp.int32, (C, C), 0)
    col = lax.broadcasted_iota(jnp.int32, (C, C), 1)
    causal = row >= col
    strict = row > col

    hs = range(n_qk)
    hvs = range(n_v)

    def phase1(c, carry):
        rows = pl.ds(pl.multiple_of(c * C, C), C)
        rows2 = pl.ds(pl.multiple_of(c * 2 * C, 2 * C), C)
        rows2b = pl.ds(pl.multiple_of(c * 2 * C, 2 * C) + C, C)
        gbc = gb_ref[rows, :]
        q = [q_ref[rows, h * hd:(h + 1) * hd] for h in hs]
        k = [k_ref[rows, h * hd:(h + 1) * hd] for h in hs]
        kk = [_dot_nt(k[h], k[h]) for h in hs]
        qk = [_dot_nt(q[h], k[h]) for h in hs]
        bcol = [gbc[:, hv:hv + 1] for hv in hvs]
        gcol = [gbc[:, n_v + hv:n_v + hv + 1] for hv in hvs]
        glast = [gbc[C - 1:C, n_v + hv:n_v + hv + 1] for hv in hvs]
        decay = [jnp.exp(jnp.where(causal, gcol[hv] - gcr_ref[0, hv, pl.ds(c, 1), :], -jnp.inf)) for hv in hvs]
        bk = [jnp.where(strict, kk[hv // rep] * bcol[hv] * decay[hv], 0.0) * -1.0 for hv in hvs]
        nn = list(bk)
        egc = None
        p = 1
        while 2 * p < C:
            bb = [x.astype(BF16) for x in bk]
            bk = [_dot(x, x) for x in bb]
            if egc is None:
                egc = [jnp.exp(g) for g in gcol]
                for hv in hvs:
                    qf = q[hv // rep].astype(F32)
                    kf = k[hv // rep].astype(F32)
                    wq_s[hv, rows2b, :] = (qf * egc[hv]).astype(BF16)
                    kd = kf * jnp.exp(glast[hv] - gcol[hv])
                    qkd_s[hv, c, 0:C, :] = (qk[hv // rep] * decay[hv]).astype(BF16)
                    qkd_s[hv, c, C:3 * C, :] = kd.T.astype(BF16)
                    gl_s[hv, pl.ds(c, 1), :] = jnp.broadcast_to(jnp.exp(glast[hv]), (1, LANES))
            nn = [n + b + _dot(n.astype(BF16), b.astype(BF16)) for n, b in zip(nn, bk)]
            p *= 2
        nb = [n.astype(BF16) for n in nn]
        vb = [v_ref[rows, hv * hd:(hv + 1) * hd].astype(F32) * bcol[hv] for hv in hvs]
        kbg = [k[hv // rep].astype(F32) * (bcol[hv] * egc[hv]) for hv in hvs]
        uu = [_dot(nb[hv], vb[hv].astype(BF16)) for hv in hvs]
        ww = [_dot(nb[hv], kbg[hv].astype(BF16)) for hv in hvs]
        for hv in hvs:
            u_s[hv, rows, :] = vb[hv] + uu[hv]
            wq_s[hv, rows2, :] = (kbg[hv] + ww[hv]).astype(BF16)
        return carry

    lax.fori_loop(0, n_chunks, phase1, 0)

    def phase2(c, carry):
        rows = pl.ds(pl.multiple_of(c * C, C), C)
        rows_wq = pl.ds(pl.multiple_of(c * 2 * C, 2 * C), 2 * C)
        sb = [st_s[hv].astype(BF16) for hv in hvs]
        ws = [_dot(wq_s[hv, rows_wq, :], sb[hv]) for hv in hvs]
        vnb = [(u_s[hv, rows, :] - ws[hv][0:C]).astype(BF16) for hv in hvs]
        ov = [_dot(qkd_s[hv, c], vnb[hv]) for hv in hvs]
        for hv in hvs:
            st_s[hv] = st_s[hv] * gl_s[hv, pl.ds(c, 1), :] + ov[hv][C:3 * C]
        for hv in hvs:
            o = ws[hv][C:2 * C] + ov[hv][0:C]
            o = o * lax.rsqrt(jnp.mean(o * o, axis=-1, keepdims=True) + NORM_EPS) * gn_ref[...]
            o = o * _silu(z_ref[rows, hv * hd:(hv + 1) * hd].astype(F32))
            o_ref[rows, hv * hd:(hv + 1) * hd] = o.astype(o_ref.dtype)
        return carry

    lax.fori_loop(0, n_chunks, phase2, 0)


def gdn_core(qn, kn, vv, proj, gb, gcr, gnorm, batch, seq, sb):
    T = qn.shape[0]
    n_qk, n_v, hd, C = GDN_QK_HEADS, GDN_V_HEADS, GDN_HEAD_DIM, GDN_CHUNK
    nc = sb // C
    nsb = seq // sb
    zc = COL_Z // (n_v * hd)
    body = functools.partial(_gdn_body, n_chunks=nc, n_qk=n_qk, n_v=n_v)
    blk = lambda b, s: (b * nsb + s, 0)
    return pl.pallas_call(
        body,
        grid=(batch, nsb),
        in_specs=[pl.BlockSpec((sb, n_qk * hd), blk),
                  pl.BlockSpec((sb, n_qk * hd), blk),
                  pl.BlockSpec((sb, n_v * hd), blk),
                  pl.BlockSpec((sb, n_v * hd), lambda b, s: (b * nsb + s, zc)),
                  pl.BlockSpec((sb, LANES), blk),
                  pl.BlockSpec((1, n_v, nc, C), lambda b, s: (b, 0, s, 0)),
                  pl.BlockSpec((1, hd), lambda b, s: (0, 0))],
        out_specs=pl.BlockSpec((sb, n_v * hd), blk),
        out_shape=jax.ShapeDtypeStruct((T, n_v * hd), BF16),
        scratch_shapes=[pltpu.VMEM((n_v, sb, hd), F32),
                        pltpu.VMEM((n_v, 2 * sb, hd), BF16),
                        pltpu.VMEM((n_v, nc, 3 * C, C), BF16),
                        pltpu.VMEM((n_v, nc, LANES), F32),
                        pltpu.VMEM((n_v, hd, hd), F32)],
        compiler_params=_cparams(2),
        name="gdn_core",
    )(qn, kn, vv, proj, gb, gcr, gnorm)


def _swa_body(q_ref, kp_ref, kc_ref, vp_ref, vc_ref, bias_ref, sink_ref, o_ref):
    W, dh = SWA_WINDOW, SWA_HEAD_DIM
    pairs = (SWA_Q_HEADS // SWA_KV_HEADS) // 2
    kvs = range(SWA_KV_HEADS)
    zero = jnp.zeros((2 * W, dh), BF16)
    one = jnp.ones((2 * W, dh), BF16)
    low = lax.broadcasted_iota(jnp.int32, (1, LANES), 1) < dh

    logits = []
    for hk in kvs:
        ks = slice(hk * dh, (hk + 1) * dh)
        kb = jnp.concatenate([kp_ref[:, ks], kc_ref[:, ks]], axis=0)
        krhs = jnp.concatenate([jnp.concatenate([kb, zero], axis=1),
                                jnp.concatenate([zero, kb], axis=1)], axis=0)
        qs = jnp.concatenate([q_ref[:, (hk * pairs + p) * LANES:(hk * pairs + p + 1) * LANES]
                              for p in range(pairs)], axis=0)
        qs = qs * jnp.asarray(dh ** -0.5, BF16)
        logits.append(_dot_nt(qs, krhs) + bias_ref[0, hk])

    outs, sinks = [], []
    for hk in kvs:
        ks = slice(hk * dh, (hk + 1) * dh)
        vb = jnp.concatenate([vp_ref[:, ks], vc_ref[:, ks]], axis=0)
        vrhs = jnp.concatenate([jnp.concatenate([vb, zero, one, zero], axis=1),
                                jnp.concatenate([zero, vb, zero, one], axis=1)], axis=0)
        ps, ms = [], []
        for c in range(2):
            lg = logits[hk][:, c * 2 * W:(c + 1) * 2 * W]
            sk = sink_ref[hk, c]
            m = jnp.max(jnp.maximum(jnp.maximum(lg[:, :W], lg[:, W:]), sk), axis=-1, keepdims=True)
            ps.append(jnp.exp(lg - m).astype(BF16))
            ms.append(sk - m)
        outs.append(_dot(jnp.concatenate(ps, axis=1), vrhs))
        sinks.append(jnp.exp(jnp.where(low, ms[0], ms[1])))

    for hk in kvs:
        out = outs[hk][:, :LANES] / (outs[hk][:, LANES:] + sinks[hk])
        for p in range(pairs):
            o_ref[:, (hk * pairs + p) * LANES:(hk * pairs + p + 1) * LANES] = (
                out[p * W:(p + 1) * W].astype(o_ref.dtype))


def swa_core(proj, bias_tbl, sink_tbl, batch, seq):
    T = proj.shape[0]
    W = SWA_WINDOW
    nb = seq // W
    qc, kc, vc = COL_SQ // SWA_Q_DIM, COL_SK // SWA_KV_DIM, COL_SV // SWA_KV_DIM
    cur = lambda b, n: b * nb + n
    prev = lambda b, n: jnp.maximum(b * nb + n - 1, 0)
    return pl.pallas_call(
        _swa_body,
        grid=(batch, nb),
        in_specs=[pl.BlockSpec((W, SWA_Q_DIM), lambda b, n: (cur(b, n), qc)),
                  pl.BlockSpec((W, SWA_KV_DIM), lambda b, n: (prev(b, n), kc)),
                  pl.BlockSpec((W, SWA_KV_DIM), lambda b, n: (cur(b, n), kc)),
                  pl.BlockSpec((W, SWA_KV_DIM), lambda b, n: (prev(b, n), vc)),
                  pl.BlockSpec((W, SWA_KV_DIM), lambda b, n: (cur(b, n), vc)),
                  pl.BlockSpec((1,) + bias_tbl.shape[1:], lambda b, n: (jnp.minimum(n, 1), 0, 0, 0)),
                  pl.BlockSpec(sink_tbl.shape, lambda b, n: (0, 0, 0, 0))],
        out_specs=pl.BlockSpec((W, SWA_Q_DIM), lambda b, n: (cur(b, n), 0)),
        out_shape=jax.ShapeDtypeStruct((T, SWA_Q_DIM), BF16),
        compiler_params=_cparams(2),
        name="swa_core",
    )(proj, proj, proj, proj, proj, bias_tbl, sink_tbl)


def _t5_bucket(dist):
    max_exact = REL_BUCKETS // 2
    large = max_exact + (jnp.log(jnp.maximum(dist, 1).astype(F32) / max_exact)
                         / math.log(REL_MAX_DIST / max_exact) * (REL_BUCKETS - max_exact)).astype(jnp.int32)
    large = jnp.minimum(large, REL_BUCKETS - 1)
    return jnp.where(dist < max_exact, dist, large)


def swa_tables(rel_bias, sinks):
    W, hq, hkv = SWA_WINDOW, SWA_Q_HEADS, SWA_KV_HEADS
    pairs = (hq // hkv) // 2
    by_dist = rel_bias.astype(F32)[_t5_bucket(jnp.arange(W))]
    period = 3 * W
    vec = jnp.concatenate([jnp.full((1, hq), MASKED, F32), by_dist[::-1],
                           jnp.full((period - W - 1, hq), MASKED, F32)], axis=0).T
    tbl = jnp.tile(vec, (1, W))[:, :W * (period - 1)].reshape(hq, W, period - 1)[:, :, :2 * W]
    first = jnp.where(jnp.arange(2 * W)[None, None, :] < W, MASKED, tbl)
    tbl = jnp.stack([first, tbl])
    tbl = tbl.reshape(2, hkv, pairs, 2, W, 2 * W).transpose(0, 1, 2, 4, 3, 5)
    tbl = tbl.reshape(2, hkv, pairs * W, 4 * W)
    sk = sinks.astype(F32).reshape(hkv, pairs, 2).transpose(0, 2, 1)
    sk = jnp.broadcast_to(sk[:, :, :, None, None], (hkv, 2, pairs, W, LANES)).reshape(hkv, 2, pairs * W, LANES)
    return tbl, sk


def _merge_body(oa_ref, ob_ref, wa_ref, wb_ref, ga_ref, gb_ref, y_ref):
    a = _dot(oa_ref[...], wa_ref[...])
    b = _dot(ob_ref[...], wb_ref[...])
    y = jax.nn.sigmoid(ga_ref[...].astype(F32)) * a + jax.nn.sigmoid(gb_ref[...].astype(F32)) * b
    y_ref[...] = y.astype(y_ref.dtype)


def merge_proj(o_a, o_b, w_a, w_b, proj, tm, tn):
    T, K = o_a.shape
    N = w_a.shape[1]
    ga0, gb0 = COL_GA // tn, COL_GB // tn
    return pl.pallas_call(
        _merge_body,
        grid=(T // tm, N // tn),
        in_specs=[pl.BlockSpec((tm, K), lambda i, j: (i, 0)),
                  pl.BlockSpec((tm, K), lambda i, j: (i, 0)),
                  pl.BlockSpec((K, tn), lambda i, j: (0, j)),
                  pl.BlockSpec((K, tn), lambda i, j: (0, j)),
                  pl.BlockSpec((tm, tn), lambda i, j: (i, ga0 + j)),
                  pl.BlockSpec((tm, tn), lambda i, j: (i, gb0 + j))],
        out_specs=pl.BlockSpec((tm, tn), lambda i, j: (i, j)),
        out_shape=jax.ShapeDtypeStruct((T, N), BF16),
        compiler_params=_cparams(2),
        name="merge_proj",
    )(o_a, o_b, w_a, w_b, proj, proj)


def _resid_body(y_ref, w_ref, x_ref, o_ref):
    o_ref[...] = x_ref[...] + _dot(y_ref[...], w_ref[...])


def out_proj(y, w, x2, tm, tn):
    T, K = y.shape
    N = w.shape[1]
    return pl.pallas_call(
        _resid_body,
        grid=(T // tm, N // tn),
        in_specs=[pl.BlockSpec((tm, K), lambda i, j: (i, 0)),
                  pl.BlockSpec((K, tn), lambda i, j: (0, j)),
                  pl.BlockSpec((tm, tn), lambda i, j: (i, j))],
        out_specs=pl.BlockSpec((tm, tn), lambda i, j: (i, j)),
        out_shape=jax.ShapeDtypeStruct((T, N), F32),
        compiler_params=_cparams(2),
        name="out_proj",
    )(y, w, x2)


def _ffn_body(x_ref, g_ref, wg_ref, wu_ref, wd_ref, o_ref, h_ref):
    @pl.when(pl.program_id(1) == 0)
    def _():
        x = x_ref[...]
        h_ref[...] = _rms(x, g_ref[...]).astype(BF16)
        o_ref[...] = x

    h = h_ref[...]
    mid = (_silu(_dot(h, wg_ref[...])) * _dot(h, wu_ref[...])).astype(BF16)
    o_ref[...] += _dot(mid, wd_ref[...])


def dense_ffn(x2, g, w_gate, w_up, w_down, tm, tf):
    T, D = x2.shape
    F = w_gate.shape[1]
    return pl.pallas_call(
        _ffn_body,
        grid=(T // tm, F // tf),
        in_specs=[pl.BlockSpec((tm, D), lambda i, f: (i, 0)),
                  pl.BlockSpec((1, D), lambda i, f: (0, 0)),
                  pl.BlockSpec((D, tf), lambda i, f: (0, f)),
                  pl.BlockSpec((D, tf), lambda i, f: (0, f)),
                  pl.BlockSpec((tf, D), lambda i, f: (f, 0))],
        out_specs=pl.BlockSpec((tm, D), lambda i, f: (i, 0)),
        out_shape=jax.ShapeDtypeStruct((T, D), F32),
        scratch_shapes=[pltpu.VMEM((tm, D), BF16)],
        compiler_params=_cparams(2),
        name="dense_ffn",
    )(x2, g, w_gate, w_up, w_down)


def _router_body(x_ref, g_ref, rwt_ref, e_ref, w_ref):
    h = _rms(x_ref[...], g_ref[...]).astype(BF16)
    lt = _dot_nt(rwt_ref[...], h)
    idx = lax.broadcasted_iota(jnp.int32, lt.shape, 0)
    m1 = jnp.max(lt, axis=0, keepdims=True)
    i1 = jnp.min(jnp.where(lt == m1, idx, N_EXPERTS), axis=0, keepdims=True)
    lt2 = jnp.where(idx == i1, -jnp.inf, lt)
    m2 = jnp.max(lt2, axis=0, keepdims=True)
    i2 = jnp.min(jnp.where(lt2 == m2, idx, N_EXPERTS), axis=0, keepdims=True)
    e2 = jnp.exp(m2 - m1)
    den = 1.0 + e2
    e_ref[...] = jnp.concatenate([i1, i2], axis=0)
    w_ref[...] = jnp.concatenate([1.0 / den, e2 / den], axis=0)


def moe_router(x2, g, rw_t, tm):
    T, D = x2.shape
    return pl.pallas_call(
        _router_body,
        grid=(T // tm,),
        in_specs=[pl.BlockSpec((tm, D), lambda i: (i, 0)),
                  pl.BlockSpec((1, D), lambda i: (0, 0)),
                  pl.BlockSpec((N_EXPERTS, D), lambda i: (0, 0))],
        out_specs=[pl.BlockSpec((TOP_K, tm), lambda i: (0, i)),
                   pl.BlockSpec((TOP_K, tm), lambda i: (0, i))],
        out_shape=[jax.ShapeDtypeStruct((TOP_K, T), jnp.int32),
                   jax.ShapeDtypeStruct((TOP_K, T), F32)],
        compiler_params=_cparams(1),
        name="moe_router",
    )(x2, g, rw_t)


def _moe_body(be_ref, nv_ref, nu_ref, tok_ref, x_hbm, g_ref, wg_ref, wu_ref, wd_ref, o_ref,
              xbuf, h_ref, sem, *, tm):
    i = pl.program_id(0)
    f = pl.program_id(1)
    n_used = nu_ref[0]
    hm = tm // 2

    def row_copy(idx, r):
        return pltpu.make_async_copy(x_hbm.at[pl.ds(idx, 1), :], xbuf.at[pl.ds(r, 1), :], sem.at[0])

    def gather(blk):
        def issue(r8, c):
            for u in range(DMA_UNROLL):
                r = r8 * DMA_UNROLL + u
                row_copy(tok_ref[blk * tm + r], r).start()
            return c
        n_rows = jnp.where(nv_ref[blk] > hm, tm, hm)
        lax.fori_loop(0, n_rows // DMA_UNROLL, issue, 0)

    def wait_rows(n_rows):
        for r in range(n_rows):
            row_copy(0, r).wait()

    def compute(rows):
        h = h_ref[0:rows, :]
        gate = _dot(h, wg_ref[...].astype(BF16))
        up = _dot(h, wu_ref[...].astype(BF16))
        mid = (_silu(gate) * up).astype(BF16)
        o_ref[0:rows, :] += _dot(mid, wd_ref[...].astype(BF16))

    active = i < n_used
    full = nv_ref[i] > hm

    @pl.when((f == 0) & (i == 0))
    def _():
        gather(0)

    @pl.when((f == 0) & active & full)
    def _():
        wait_rows(tm)
        h_ref[...] = _rms(xbuf[...], g_ref[...]).astype(BF16)

    @pl.when((f == 0) & active & jnp.logical_not(full))
    def _():
        wait_rows(hm)
        h_ref[0:hm, :] = _rms(xbuf[0:hm, :], g_ref[...]).astype(BF16)

    @pl.when((f == 1) & (i + 1 < n_used))
    def _():
        gather(i + 1)

    @pl.when(f == 0)
    def _():
        o_ref[...] = jnp.zeros_like(o_ref)

    @pl.when(active & full)
    def _():
        compute(tm)

    @pl.when(active & jnp.logical_not(full))
    def _():
        compute(hm)


def moe_experts(x2, g, w_gate, w_up, w_down, block_e, n_valid, n_used, row_tok, tm, tf):
    T, D = x2.shape
    E, _, F = w_gate.shape
    n_rows = row_tok.shape[0]
    nf = F // tf
    assert nf >= 2

    def fidx(i, f, nu):
        return jnp.where(i < nu[0], f, nf - 1)

    body = functools.partial(_moe_body, tm=tm)
    return pl.pallas_call(
        body,
        grid_spec=pltpu.PrefetchScalarGridSpec(
            num_scalar_prefetch=4,
            grid=(n_rows // tm, nf),
            in_specs=[pl.BlockSpec(memory_space=pl.ANY),
                      pl.BlockSpec((1, D), lambda i, f, be, nv, nu, tok: (0, 0)),
                      pl.BlockSpec((None, D, tf), lambda i, f, be, nv, nu, tok: (be[i], 0, fidx(i, f, nu))),
                      pl.BlockSpec((None, D, tf), lambda i, f, be, nv, nu, tok: (be[i], 0, fidx(i, f, nu))),
                      pl.BlockSpec((None, tf, D), lambda i, f, be, nv, nu, tok: (be[i], fidx(i, f, nu), 0))],
            out_specs=pl.BlockSpec((tm, D), lambda i, f, be, nv, nu, tok: (i, 0)),
            scratch_shapes=[pltpu.VMEM((tm, D), F32),
                            pltpu.VMEM((tm, D), BF16),
                            pltpu.SemaphoreType.DMA((1,))]),
        out_shape=jax.ShapeDtypeStruct((n_rows, D), F32),
        compiler_params=_cparams(2),
        name="moe_experts",
    )(block_e, n_valid, n_used, row_tok, x2, g, w_gate, w_up, w_down)


def _combine_body(dest_ref, yg_hbm, x_ref, tw_ref, g_ref, o_ref, ybuf, sem, *, tt, n_steps, final_norm):
    i = pl.program_id(0)

    def gather(blk, slot):
        def issue(r8, c):
            for u in range(DMA_UNROLL):
                r = r8 * DMA_UNROLL + u
                for k in range(TOP_K):
                    pltpu.make_async_copy(yg_hbm.at[pl.ds(dest_ref[(blk * tt + r) * TOP_K + k], 1), :],
                                          ybuf.at[slot, k, pl.ds(r, 1), :], sem.at[slot]).start()
            return c
        lax.fori_loop(0, tt // DMA_UNROLL, issue, 0)

    @pl.when(i == 0)
    def _():
        gather(0, 0)

    slot = i % 2

    @pl.when(i + 1 < n_steps)
    def _():
        gather(i + 1, 1 - slot)

    for r in range(tt):
        for k in range(TOP_K):
            pltpu.make_async_copy(yg_hbm.at[pl.ds(0, 1), :], ybuf.at[slot, k, pl.ds(r, 1), :],
                                  sem.at[slot]).wait()

    tw = tw_ref[...]
    y = x_ref[...] + ybuf[slot, 0] * tw[:, 0:1] + ybuf[slot, 1] * tw[:, 1:2]
    if final_norm:
        y = _rms(y, g_ref[...])
    o_ref[...] = y


def moe_combine(yg, x2, tw_col, dest, g_final, tt, final_norm):
    T, D = x2.shape
    n_steps = T // tt
    body = functools.partial(_combine_body, tt=tt, n_steps=n_steps, final_norm=final_norm)
    return pl.pallas_call(
        body,
        grid_spec=pltpu.PrefetchScalarGridSpec(
            num_scalar_prefetch=1,
            grid=(n_steps,),
            in_specs=[pl.BlockSpec(memory_space=pl.ANY),
                      pl.BlockSpec((tt, D), lambda i, d: (i, 0)),
                      pl.BlockSpec((tt, LANES), lambda i, d: (i, 0)),
                      pl.BlockSpec((1, D), lambda i, d: (0, 0))],
            out_specs=pl.BlockSpec((tt, D), lambda i, d: (i, 0)),
            scratch_shapes=[pltpu.VMEM((2, TOP_K, tt, D), F32),
                            pltpu.SemaphoreType.DMA((2,))]),
        out_shape=jax.ShapeDtypeStruct((T, D), F32),
        compiler_params=_cparams(1),
        name="moe_combine",
    )(dest, yg, x2, tw_col, g_final)


def moe_plan(top_e, tm):
    T = top_e.shape[1]
    TK = T * TOP_K
    experts = jnp.arange(N_EXPERTS, dtype=jnp.int32)
    rows = TK // LANES
    onehot = (top_e.T.reshape(1, rows, LANES) == experts[:, None, None]).astype(F32)
    tri = (jnp.arange(LANES)[:, None] <= jnp.arange(LANES)[None, :]).astype(F32)
    within = jnp.einsum('erk,kl->erl', onehot, tri, precision=lax.Precision.HIGHEST)
    before = (jnp.arange(rows)[:, None] < jnp.arange(rows)[None, :]).astype(F32)
    row_off = jnp.einsum('er,rs->es', within[:, :, -1], before, precision=lax.Precision.HIGHEST)
    cum = within + row_off[:, :, None]
    counts = cum[:, -1, -1].astype(jnp.int32)
    padded = (counts + tm - 1) // tm * tm
    pad_end = jnp.cumsum(padded)
    pad_start = pad_end - padded
    dest = jnp.sum(onehot * (cum - 1.0 + pad_start.astype(F32)[:, None, None]), axis=0)
    dest = dest.reshape(TK).astype(jnp.int32)
    n_rows = TK + N_EXPERTS * tm
    n_blocks = n_rows // tm
    row_tok = jnp.zeros((n_rows,), jnp.int32).at[dest].set(jnp.arange(TK, dtype=jnp.int32) // TOP_K)
    blk = jnp.arange(n_blocks, dtype=jnp.int32)
    blk_onehot = ((blk[:, None] * tm >= pad_start[None, :]) & (blk[:, None] * tm < pad_end[None, :])).astype(jnp.int32)
    n_used = (pad_end[-1:] // tm).astype(jnp.int32)
    last_e = jnp.max(jnp.where(padded > 0, experts, 0))
    block_e = jnp.where(blk < n_used[0], jnp.sum(blk_onehot * experts[None, :], axis=1), last_e)
    n_valid = jnp.clip(jnp.sum(blk_onehot * (counts + pad_start)[None, :], axis=1) - blk * tm, 0, tm)
    return dest.astype(jnp.int32), row_tok, block_e.astype(jnp.int32), n_valid.astype(jnp.int32), n_used


def moe_block(x2, g, router_w, w_gate, w_up, w_down, g_final, final_norm, tm_r, tm_e, tf, tt):
    top_e, top_w = moe_router(x2, g, router_w.T.astype(BF16), tm_r)
    dest, row_tok, block_e, n_valid, n_used = moe_plan(top_e, tm_e)
    yg = moe_experts(x2, g, w_gate, w_up, w_down, block_e, n_valid, n_used, row_tok, tm_e, tf)
    tw_col = jnp.pad(top_w.T, ((0, 0), (0, LANES - TOP_K)))
    return moe_combine(yg, x2, tw_col, dest, g_final, tt, final_norm)


def _norm_body(x_ref, g_ref, o_ref):
    o_ref[...] = _rms(x_ref[...], g_ref[...])


def final_norm(x2, g, tm):
    T, D = x2.shape
    return pl.pallas_call(
        _norm_body,
        grid=(T // tm,),
        in_specs=[pl.BlockSpec((tm, D), lambda i: (i, 0)), pl.BlockSpec((1, D), lambda i: (0, 0))],
        out_specs=pl.BlockSpec((tm, D), lambda i: (i, 0)),
        out_shape=jax.ShapeDtypeStruct((T, D), F32),
        compiler_params=_cparams(1),
        name="final_norm",
    )(x2, g)


def _split_w_in(w):
    sizes = (GDN_KEY_DIM, GDN_KEY_DIM, GDN_VAL_DIM, GDN_VAL_DIM, GDN_V_HEADS, GDN_V_HEADS,
             SWA_Q_DIM, SWA_KV_DIM, SWA_KV_DIM, D_MODEL, D_MODEL)
    offs = [0]
    for s in sizes:
        offs.append(offs[-1] + s)
    q, k, v, z, b, a, sq, sk, sv, ga, gb = [w[:, offs[n]:offs[n + 1]] for n in range(len(sizes))]
    w_main = jnp.concatenate([q, k, v, z, sq, ga, gb, sk, sv], axis=1).astype(BF16)
    w_ba = jnp.pad(jnp.concatenate([b, a], axis=1), ((0, 0), (0, LANES - 2 * GDN_V_HEADS))).astype(BF16)
    return w_main, w_ba


def _pad_lanes(vec, offset):
    return jnp.pad(vec.astype(F32), (offset, LANES - offset - vec.shape[0])).reshape(1, LANES)


def mixer_layer(x2, batch, seq, ln_g, w_in, conv_w, a_log, dt_bias, gdn_norm_g, bias_tbl, sink_tbl,
                w_proj_gdn, w_proj_swa, w_out):
    T = x2.shape[0]
    w_main, w_ba = _split_w_in(w_in)
    proj, ba = in_proj(x2, ln_g.reshape(1, -1), w_main, w_ba, tm=1024, tn=1280)
    qn, kn, vv, gb = gdn_prep(proj, ba, conv_w, _pad_lanes(a_log, GDN_V_HEADS),
                              _pad_lanes(dt_bias, GDN_V_HEADS), seq, ts=256)
    n_chunks = seq // GDN_CHUNK
    gcr = gb[:, GDN_V_HEADS:2 * GDN_V_HEADS].reshape(batch, n_chunks, GDN_CHUNK, GDN_V_HEADS)
    gcr = gcr.transpose(0, 3, 1, 2)
    o_a = gdn_core(qn, kn, vv, proj, gb, gcr, gdn_norm_g.reshape(1, -1).astype(F32), batch, seq, sb=512)
    o_b = swa_core(proj, bias_tbl, sink_tbl, batch, seq)
    y = merge_proj(o_a, o_b, w_proj_gdn.astype(BF16), w_proj_swa.astype(BF16), proj, tm=1024, tn=512)
    return out_proj(y, w_out.astype(BF16), x2, tm=1024, tn=512)


def kernel(x, ln_mix_g, w_in, conv_w, a_log, dt_bias, gdn_norm_g, sinks, rel_bias, w_proj_gdn, w_proj_swa,
           w_out, ln_ffn_g, ffn_w_gate, ffn_w_up, ffn_w_down, router_w, moe_w_gate, moe_w_up, moe_w_down,
           ln_final_g):
    B, S, D = x.shape
    depth = w_in.shape[0]
    x2 = x.reshape(B * S, D)
    g_final = ln_final_g.reshape(1, D)
    normed = False
    for i in range(depth):
        bias_tbl, sink_tbl = swa_tables(rel_bias, sinks[i])
        x2 = mixer_layer(x2, B, S, ln_mix_g[i], w_in[i], conv_w[i], a_log[i], dt_bias[i], gdn_norm_g[i],
                         bias_tbl, sink_tbl, w_proj_gdn[i], w_proj_swa[i], w_out[i])
        g_ffn = ln_ffn_g[i].reshape(1, D)
        j = i // 2
        if i % 2 == 0:
            f = ffn_w_gate.shape[2]
            fpad = (-f) % 512
            wg = jnp.pad(ffn_w_gate[j], ((0, 0), (0, fpad))).astype(BF16)
            wu = jnp.pad(ffn_w_up[j], ((0, 0), (0, fpad))).astype(BF16)
            wd = jnp.pad(ffn_w_down[j], ((0, fpad), (0, 0))).astype(BF16)
            x2 = dense_ffn(x2, g_ffn, wg, wu, wd, tm=1024, tf=512)
        else:
            last = i == depth - 1
            x2 = moe_block(x2, g_ffn, router_w[j], moe_w_gate[j], moe_w_up[j], moe_w_down[j],
                           g_final, last, tm_r=512, tm_e=1024, tf=256, tt=256)
            normed = last
    if not normed:
        x2 = final_norm(x2, g_final, tm=512)
    return x2.reshape(B, S, D)
```

```python
import functools
import math

import jax
import jax.numpy as jnp
from jax import lax
from jax.experimental import pallas as pl
from jax.experimental.pallas import tpu as pltpu

F32 = jnp.float32
BF16 = jnp.bfloat16

D_MODEL = 2048
GDN_QK_HEADS = 8
GDN_V_HEADS = 16
GDN_HEAD_DIM = 128
GDN_CONV = 4
GDN_CHUNK = 64
SWA_Q_HEADS = 32
SWA_KV_HEADS = 4
SWA_HEAD_DIM = 64
SWA_WINDOW = 128
REL_BUCKETS = 32
REL_MAX_DIST = 128
N_EXPERTS = 8
TOP_K = 2
NORM_EPS = 1e-6
GDN_KEY_DIM = GDN_QK_HEADS * GDN_HEAD_DIM
GDN_VAL_DIM = GDN_V_HEADS * GDN_HEAD_DIM
SWA_Q_DIM = SWA_Q_HEADS * SWA_HEAD_DIM
SWA_KV_DIM = SWA_KV_HEADS * SWA_HEAD_DIM

LANES = 128
VMEM_LIMIT = 56 * 1024 * 1024

COL_Q = 0
COL_K = COL_Q + GDN_KEY_DIM
COL_V = COL_K + GDN_KEY_DIM
COL_Z = COL_V + GDN_VAL_DIM
COL_SQ = COL_Z + GDN_VAL_DIM
COL_SK = COL_SQ + SWA_Q_DIM
COL_SV = COL_SK + SWA_KV_DIM
COL_GA = COL_SV + SWA_KV_DIM
COL_GB = COL_GA + D_MODEL
PROJ_COLS = COL_GB + D_MODEL
BA_COLS = 2 * GDN_V_HEADS
GDN_CONV_DIM = 2 * GDN_KEY_DIM + GDN_VAL_DIM

MASKED = -1e30
DMA_UNROLL = 8


def _cparams(n_axes):
    return pltpu.CompilerParams(dimension_semantics=("arbitrary",) * n_axes,
                                vmem_limit_bytes=VMEM_LIMIT)


def _rms(x, g):
    return x * lax.rsqrt(jnp.mean(x * x, axis=-1, keepdims=True) + NORM_EPS) * g


def _silu(x):
    return x * jax.nn.sigmoid(x)


def _dot(a, b):
    return jnp.dot(a, b, preferred_element_type=F32)


def _dot_nt(a, b):
    return lax.dot_general(a, b, (((1,), (1,)), ((), ())), preferred_element_type=F32)


def _dot_tn(a, b):
    return lax.dot_general(a, b, (((0,), (0,)), ((), ())), preferred_element_type=F32)


def _inproj_body(x_ref, g_ref, w_ref, wba_ref, o_ref, oba_ref, h_ref):
    @pl.when(pl.program_id(1) == 0)
    def _():
        h = _rms(x_ref[...], g_ref[...]).astype(BF16)
        h_ref[...] = h
        oba_ref[...] = _dot(h, wba_ref[...])

    o_ref[...] = _dot(h_ref[...], w_ref[...]).astype(o_ref.dtype)


def in_proj(x2, g, w_main, w_ba, layer, tm, tn):
    T, D = x2.shape
    N = w_main.shape[2]
    return pl.pallas_call(
        _inproj_body,
        grid=(T // tm, N // tn),
        in_specs=[pl.BlockSpec((tm, D), lambda i, j: (i, 0)),
                  pl.BlockSpec((1, D), lambda i, j: (0, 0)),
                  pl.BlockSpec((None, D, tn), lambda i, j: (layer, 0, j)),
                  pl.BlockSpec((None, D, LANES), lambda i, j: (layer, 0, 0))],
        out_specs=[pl.BlockSpec((tm, tn), lambda i, j: (i, j)),
                   pl.BlockSpec((tm, LANES), lambda i, j: (i, 0))],
        out_shape=[jax.ShapeDtypeStruct((T, N), BF16), jax.ShapeDtypeStruct((T, LANES), F32)],
        scratch_shapes=[pltpu.VMEM((tm, D), BF16)],
        compiler_params=_cparams(2),
        name="in_proj",
    )(x2, g, w_main, w_ba)


def _prep_body(x_ref, halo_ref, cw_ref, ba_ref, alog_ref, dtb_ref, q_ref, k_ref, v_ref, gb_ref,
               *, ts, seq, n_qk, n_v, chunk):
    first = (pl.program_id(0) * ts) % seq == 0
    for c in range(2 * n_qk + n_v):
        cs = slice(c * LANES, (c + 1) * LANES)
        cur = x_ref[:, cs].astype(F32)
        hal = halo_ref[8:16, cs].astype(F32)
        hal = jnp.where(first, 0.0, hal)
        ext = jnp.concatenate([hal, cur], axis=0)
        y = ext[5:5 + ts] * cw_ref[0:1, cs]
        y = y + ext[6:6 + ts] * cw_ref[1:2, cs]
        y = y + ext[7:7 + ts] * cw_ref[2:3, cs]
        y = y + cur * cw_ref[3:4, cs]
        y = _silu(y)
        if c < 2 * n_qk:
            y = y * lax.rsqrt(jnp.sum(y * y, axis=-1, keepdims=True) + 1e-6)
        if c < n_qk:
            q_ref[:, cs] = (y * (GDN_HEAD_DIM ** -0.5)).astype(q_ref.dtype)
        elif c < 2 * n_qk:
            k_ref[:, (c - n_qk) * LANES:(c - n_qk + 1) * LANES] = y.astype(k_ref.dtype)
        else:
            v_ref[:, (c - 2 * n_qk) * LANES:(c - 2 * n_qk + 1) * LANES] = y.astype(v_ref.dtype)

    ba = ba_ref[...]
    lane = lax.broadcasted_iota(jnp.int32, ba.shape, 1)
    pos = lax.broadcasted_iota(jnp.int32, ba.shape, 0) % chunk
    beta = jax.nn.sigmoid(ba)
    t = ba + dtb_ref[...]
    softplus = jnp.maximum(t, 0.0) + jnp.log1p(jnp.exp(-jnp.abs(t)))
    gc = -jnp.exp(alog_ref[...]) * softplus
    s = 1
    while s < chunk:
        gc = gc + jnp.where(pos >= s, pltpu.roll(gc, s, 0), 0.0)
        s *= 2
    gb_ref[...] = jnp.where(lane < n_v, beta, jnp.where(lane < 2 * n_v, gc, 0.0))


def gdn_prep(proj, ba, conv_w, alog_pad, dtb_pad, seq, ts):
    T = proj.shape[0]
    n_qk, n_v = GDN_QK_HEADS, GDN_V_HEADS
    body = functools.partial(_prep_body, ts=ts, seq=seq, n_qk=n_qk, n_v=n_v, chunk=GDN_CHUNK)
    return pl.pallas_call(
        body,
        grid=(T // ts,),
        in_specs=[pl.BlockSpec((ts, GDN_CONV_DIM), lambda i: (i, 0)),
                  pl.BlockSpec((16, GDN_CONV_DIM), lambda i: (jnp.maximum(i * (ts // 16) - 1, 0), 0)),
                  pl.BlockSpec((GDN_CONV, GDN_CONV_DIM), lambda i: (0, 0)),
                  pl.BlockSpec((ts, LANES), lambda i: (i, 0)),
                  pl.BlockSpec((1, LANES), lambda i: (0, 0)),
                  pl.BlockSpec((1, LANES), lambda i: (0, 0))],
        out_specs=[pl.BlockSpec((ts, GDN_KEY_DIM), lambda i: (i, 0)),
                   pl.BlockSpec((ts, GDN_KEY_DIM), lambda i: (i, 0)),
                   pl.BlockSpec((ts, GDN_VAL_DIM), lambda i: (i, 0)),
                   pl.BlockSpec((ts, LANES), lambda i: (i, 0))],
        out_shape=[jax.ShapeDtypeStruct((T, GDN_KEY_DIM), BF16),
                   jax.ShapeDtypeStruct((T, GDN_KEY_DIM), BF16),
                   jax.ShapeDtypeStruct((T, GDN_VAL_DIM), BF16),
                   jax.ShapeDtypeStruct((T, LANES), F32)],
        compiler_params=_cparams(1),
        name="gdn_prep",
    )(proj, proj, conv_w, ba, alog_pad, dtb_pad)


def _gdn_body(q_ref, k_ref, v_ref, z_ref, gb_ref, gcr_ref, gn_ref, o_ref,
              u_s, wq_s, qkd_s, gl_s, st_s, *, n_chunks, n_qk, n_v):
    C = GDN_CHUNK
    hd = GDN_HEAD_DIM
    rep = n_v // n_qk

    @pl.when(pl.program_id(1) == 0)
    def _():
        st_s[...] = jnp.zeros_like(st_s)

    row = lax.broadcasted_iota(jnp.int32, (C, C), 0)
    col = lax.broadcasted_iota(jnp.int32, (C, C), 1)
    causal = row >= col
    strict = row > col

    hs = range(n_qk)
    hvs = range(n_v)

    def phase1(c, carry):
        rows = pl.ds(pl.multiple_of(c * C, C), C)
        rows2 = pl.ds(pl.multiple_of(c * 2 * C, 2 * C), C)
        rows2b = pl.ds(pl.multiple_of(c * 2 * C, 2 * C) + C, C)
        gbc = gb_ref[rows, :]
        q = [q_ref[rows, h * hd:(h + 1) * hd] for h in hs]
        k = [k_ref[rows, h * hd:(h + 1) * hd] for h in hs]
        kk = [_dot_nt(k[h], k[h]) for h in hs]
        qk = [_dot_nt(q[h], k[h]) for h in hs]
        bcol = [gbc[:, hv:hv + 1] for hv in hvs]
        gcol = [gbc[:, n_v + hv:n_v + hv + 1] for hv in hvs]
        glast = [gbc[C - 1:C, n_v + hv:n_v + hv + 1] for hv in hvs]
        decay = [jnp.exp(jnp.where(causal, gcol[hv] - gcr_ref[0, hv, pl.ds(c, 1), :], -jnp.inf)) for hv in hvs]
        bk = [jnp.where(strict, kk[hv // rep] * bcol[hv] * decay[hv], 0.0) * -1.0 for hv in hvs]
        nn = list(bk)
        egc = None
        p = 1
        while 2 * p < C:
            bb = [x.astype(BF16) for x in bk]
            sq = [_dot(x, x) for x in bb]
            if egc is not None:
                nn = [n + b + _dot(n.astype(BF16), b.astype(BF16)) for n, b in zip(nn, bk)]
            bk = sq
            if egc is None:
                egc = [jnp.exp(g) for g in gcol]
                for hv in hvs:
                    qf = q[hv // rep].astype(F32)
                    kf = k[hv // rep].astype(F32)
                    wq_s[hv, rows2b, :] = (qf * egc[hv]).astype(BF16)
                    kd = kf * jnp.exp(glast[hv] - gcol[hv])
                    qkd_s[hv, c, 0:C, :] = (qk[hv // rep] * decay[hv]).astype(BF16)
                    qkd_s[hv, c, C:3 * C, :] = kd.T.astype(BF16)
                    gl_s[hv, pl.ds(c, 1), :] = jnp.broadcast_to(jnp.exp(glast[hv]), (1, LANES))
            p *= 2
        nn = [n + b + _dot(n.astype(BF16), b.astype(BF16)) for n, b in zip(nn, bk)]
        nb = [n.astype(BF16) for n in nn]
        vb = [v_ref[rows, hv * hd:(hv + 1) * hd].astype(F32) * bcol[hv] for hv in hvs]
        kbg = [k[hv // rep].astype(F32) * (bcol[hv] * egc[hv]) for hv in hvs]
        uu = [_dot(nb[hv], vb[hv].astype(BF16)) for hv in hvs]
        ww = [_dot(nb[hv], kbg[hv].astype(BF16)) for hv in hvs]
        for hv in hvs:
            u_s[hv, rows, :] = vb[hv] + uu[hv]
            wq_s[hv, rows2, :] = (kbg[hv] + ww[hv]).astype(BF16)
        return carry

    lax.fori_loop(0, n_chunks, phase1, 0)

    def phase2(c, carry):
        rows = pl.ds(pl.multiple_of(c * C, C), C)
        rows_wq = pl.ds(pl.multiple_of(c * 2 * C, 2 * C), 2 * C)
        sb = [st_s[hv].astype(BF16) for hv in hvs]
        ws = [_dot(wq_s[hv, rows_wq, :], sb[hv]) for hv in hvs]
        vnb = [(u_s[hv, rows, :] - ws[hv][0:C]).astype(BF16) for hv in hvs]
        ov = [_dot(qkd_s[hv, c], vnb[hv]) for hv in hvs]
        for hv in hvs:
            st_s[hv] = st_s[hv] * gl_s[hv, pl.ds(c, 1), :] + ov[hv][C:3 * C]
        for hv in hvs:
            o = ws[hv][C:2 * C] + ov[hv][0:C]
            o = o * lax.rsqrt(jnp.mean(o * o, axis=-1, keepdims=True) + NORM_EPS) * gn_ref[...]
            o = o * _silu(z_ref[rows, hv * hd:(hv + 1) * hd].astype(F32))
            o_ref[rows, hv * hd:(hv + 1) * hd] = o.astype(o_ref.dtype)
        return carry

    lax.fori_loop(0, n_chunks, phase2, 0)


def gdn_core(qn, kn, vv, proj, gb, gcr, gnorm, batch, seq, sb):
    T = qn.shape[0]
    n_qk, n_v, hd, C = GDN_QK_HEADS, GDN_V_HEADS, GDN_HEAD_DIM, GDN_CHUNK
    nc = sb // C
    nsb = seq // sb
    zc = COL_Z // (n_v * hd)
    body = functools.partial(_gdn_body, n_chunks=nc, n_qk=n_qk, n_v=n_v)
    blk = lambda b, s: (b * nsb + s, 0)
    return pl.pallas_call(
        body,
        grid=(batch, nsb),
        in_specs=[pl.BlockSpec((sb, n_qk * hd), blk),
                  pl.BlockSpec((sb, n_qk * hd), blk),
                  pl.BlockSpec((sb, n_v * hd), blk),
                  pl.BlockSpec((sb, n_v * hd), lambda b, s: (b * nsb + s, zc)),
                  pl.BlockSpec((sb, LANES), blk),
                  pl.BlockSpec((1, n_v, nc, C), lambda b, s: (b, 0, s, 0)),
                  pl.BlockSpec((1, hd), lambda b, s: (0, 0))],
        out_specs=pl.BlockSpec((sb, n_v * hd), blk),
        out_shape=jax.ShapeDtypeStruct((T, n_v * hd), BF16),
        scratch_shapes=[pltpu.VMEM((n_v, sb, hd), F32),
                        pltpu.VMEM((n_v, 2 * sb, hd), BF16),
                        pltpu.VMEM((n_v, nc, 3 * C, C), BF16),
                        pltpu.VMEM((n_v, nc, LANES), F32),
                        pltpu.VMEM((n_v, hd, hd), F32)],
        compiler_params=_cparams(2),
        name="gdn_core",
    )(qn, kn, vv, proj, gb, gcr, gnorm)


def _swa_body(q_ref, kp_ref, kc_ref, vp_ref, vc_ref, bias_ref, sink_ref, o_ref):
    W, dh = SWA_WINDOW, SWA_HEAD_DIM
    pairs = (SWA_Q_HEADS // SWA_KV_HEADS) // 2
    kvs = range(SWA_KV_HEADS)
    zero = jnp.zeros((2 * W, dh), BF16)
    one = jnp.ones((2 * W, dh), BF16)
    low = lax.broadcasted_iota(jnp.int32, (1, LANES), 1) < dh

    logits = []
    for hk in kvs:
        ks = slice(hk * dh, (hk + 1) * dh)
        kb = jnp.concatenate([kp_ref[:, ks], kc_ref[:, ks]], axis=0)
        krhs = jnp.concatenate([jnp.concatenate([kb, zero], axis=1),
                                jnp.concatenate([zero, kb], axis=1)], axis=0)
        qs = jnp.concatenate([q_ref[:, (hk * pairs + p) * LANES:(hk * pairs + p + 1) * LANES]
                              for p in range(pairs)], axis=0)
        qs = qs * jnp.asarray(dh ** -0.5, BF16)
        logits.append(_dot_nt(qs, krhs) + bias_ref[0, hk])

    outs, sinks = [], []
    for hk in kvs:
        ks = slice(hk * dh, (hk + 1) * dh)
        vb = jnp.concatenate([vp_ref[:, ks], vc_ref[:, ks]], axis=0)
        vrhs = jnp.concatenate([jnp.concatenate([vb, zero, one, zero], axis=1),
                                jnp.concatenate([zero, vb, zero, one], axis=1)], axis=0)
        ps, ms = [], []
        for c in range(2):
            lg = logits[hk][:, c * 2 * W:(c + 1) * 2 * W]
            sk = sink_ref[hk, c]
            m = jnp.max(jnp.maximum(jnp.maximum(lg[:, :W], lg[:, W:]), sk), axis=-1, keepdims=True)
            ps.append(jnp.exp(lg - m).astype(BF16))
            ms.append(sk - m)
        outs.append(_dot(jnp.concatenate(ps, axis=1), vrhs))
        sinks.append(jnp.exp(jnp.where(low, ms[0], ms[1])))

    for hk in kvs:
        out = outs[hk][:, :LANES] / (outs[hk][:, LANES:] + sinks[hk])
        for p in range(pairs):
            o_ref[:, (hk * pairs + p) * LANES:(hk * pairs + p + 1) * LANES] = (
                out[p * W:(p + 1) * W].astype(o_ref.dtype))


def swa_core(proj, bias_tbl, sink_tbl, batch, seq):
    T = proj.shape[0]
    W = SWA_WINDOW
    nb = seq // W
    qc, kc, vc = COL_SQ // SWA_Q_DIM, COL_SK // SWA_KV_DIM, COL_SV // SWA_KV_DIM
    cur = lambda b, n: b * nb + n
    prev = lambda b, n: jnp.maximum(b * nb + n - 1, 0)
    return pl.pallas_call(
        _swa_body,
        grid=(batch, nb),
        in_specs=[pl.BlockSpec((W, SWA_Q_DIM), lambda b, n: (cur(b, n), qc)),
                  pl.BlockSpec((W, SWA_KV_DIM), lambda b, n: (prev(b, n), kc)),
                  pl.BlockSpec((W, SWA_KV_DIM), lambda b, n: (cur(b, n), kc)),
                  pl.BlockSpec((W, SWA_KV_DIM), lambda b, n: (prev(b, n), vc)),
                  pl.BlockSpec((W, SWA_KV_DIM), lambda b, n: (cur(b, n), vc)),
                  pl.BlockSpec((1,) + bias_tbl.shape[1:], lambda b, n: (jnp.minimum(n, 1), 0, 0, 0)),
                  pl.BlockSpec(sink_tbl.shape, lambda b, n: (0, 0, 0, 0))],
        out_specs=pl.BlockSpec((W, SWA_Q_DIM), lambda b, n: (cur(b, n), 0)),
        out_shape=jax.ShapeDtypeStruct((T, SWA_Q_DIM), BF16),
        compiler_params=_cparams(2),
        name="swa_core",
    )(proj, proj, proj, proj, proj, bias_tbl, sink_tbl)


def _t5_bucket(dist):
    max_exact = REL_BUCKETS // 2
    large = max_exact + (jnp.log(jnp.maximum(dist, 1).astype(F32) / max_exact)
                         / math.log(REL_MAX_DIST / max_exact) * (REL_BUCKETS - max_exact)).astype(jnp.int32)
    large = jnp.minimum(large, REL_BUCKETS - 1)
    return jnp.where(dist < max_exact, dist, large)


def swa_tables(rel_bias, sinks):
    W, hq, hkv = SWA_WINDOW, SWA_Q_HEADS, SWA_KV_HEADS
    pairs = (hq // hkv) // 2
    by_dist = rel_bias.astype(F32)[_t5_bucket(jnp.arange(W))]
    period = 3 * W
    vec = jnp.concatenate([jnp.full((1, hq), MASKED, F32), by_dist[::-1],
                           jnp.full((period - W - 1, hq), MASKED, F32)], axis=0).T
    tbl = jnp.tile(vec, (1, W))[:, :W * (period - 1)].reshape(hq, W, period - 1)[:, :, :2 * W]
    first = jnp.where(jnp.arange(2 * W)[None, None, :] < W, MASKED, tbl)
    tbl = jnp.stack([first, tbl])
    tbl = tbl.reshape(2, hkv, pairs, 2, W, 2 * W).transpose(0, 1, 2, 4, 3, 5)
    tbl = tbl.reshape(2, hkv, pairs * W, 4 * W)
    sk = sinks.astype(F32).reshape(hkv, pairs, 2).transpose(0, 2, 1)
    sk = jnp.broadcast_to(sk[:, :, :, None, None], (hkv, 2, pairs, W, LANES)).reshape(hkv, 2, pairs * W, LANES)
    return tbl, sk


def _merge_body(oa_ref, ob_ref, wa_ref, wb_ref, ga_ref, gb_ref, y_ref):
    a = _dot(oa_ref[...], wa_ref[...])
    b = _dot(ob_ref[...], wb_ref[...])
    y = jax.nn.sigmoid(ga_ref[...].astype(F32)) * a + jax.nn.sigmoid(gb_ref[...].astype(F32)) * b
    y_ref[...] = y.astype(y_ref.dtype)


def merge_proj(o_a, o_b, w_a, w_b, proj, layer, tm, tn):
    T, K = o_a.shape
    N = w_a.shape[2]
    ga0, gb0 = COL_GA // tn, COL_GB // tn
    return pl.pallas_call(
        _merge_body,
        grid=(T // tm, N // tn),
        in_specs=[pl.BlockSpec((tm, K), lambda i, j: (i, 0)),
                  pl.BlockSpec((tm, K), lambda i, j: (i, 0)),
                  pl.BlockSpec((None, K, tn), lambda i, j: (layer, 0, j)),
                  pl.BlockSpec((None, K, tn), lambda i, j: (layer, 0, j)),
                  pl.BlockSpec((tm, tn), lambda i, j: (i, ga0 + j)),
                  pl.BlockSpec((tm, tn), lambda i, j: (i, gb0 + j))],
        out_specs=pl.BlockSpec((tm, tn), lambda i, j: (i, j)),
        out_shape=jax.ShapeDtypeStruct((T, N), BF16),
        compiler_params=_cparams(2),
        name="merge_proj",
    )(o_a, o_b, w_a, w_b, proj, proj)


def _resid_body(y_ref, w_ref, x_ref, o_ref):
    o_ref[...] = x_ref[...] + _dot(y_ref[...], w_ref[...])


def out_proj(y, w, x2, layer, tm, tn):
    T, K = y.shape
    N = w.shape[2]
    return pl.pallas_call(
        _resid_body,
        grid=(T // tm, N // tn),
        in_specs=[pl.BlockSpec((tm, K), lambda i, j: (i, 0)),
                  pl.BlockSpec((None, K, tn), lambda i, j: (layer, 0, j)),
                  pl.BlockSpec((tm, tn), lambda i, j: (i, j))],
        out_specs=pl.BlockSpec((tm, tn), lambda i, j: (i, j)),
        out_shape=jax.ShapeDtypeStruct((T, N), F32),
        compiler_params=_cparams(2),
        name="out_proj",
    )(y, w, x2)


def _ffn_body(x_ref, g_ref, wg_ref, wu_ref, wd_ref, o_ref, h_ref):
    @pl.when(pl.program_id(1) == 0)
    def _():
        x = x_ref[...]
        h_ref[...] = _rms(x, g_ref[...]).astype(BF16)
        o_ref[...] = x

    h = h_ref[...]
    mid = (_silu(_dot(h, wg_ref[...])) * _dot(h, wu_ref[...])).astype(BF16)
    o_ref[...] += _dot(mid, wd_ref[...])


def dense_ffn(x2, g, w_gate, w_up, w_down, tm, tf):
    T, D = x2.shape
    F = w_gate.shape[1]
    return pl.pallas_call(
        _ffn_body,
        grid=(T // tm, F // tf),
        in_specs=[pl.BlockSpec((tm, D), lambda i, f: (i, 0)),
                  pl.BlockSpec((1, D), lambda i, f: (0, 0)),
                  pl.BlockSpec((D, tf), lambda i, f: (0, f)),
                  pl.BlockSpec((D, tf), lambda i, f: (0, f)),
                  pl.BlockSpec((tf, D), lambda i, f: (f, 0))],
        out_specs=pl.BlockSpec((tm, D), lambda i, f: (i, 0)),
        out_shape=jax.ShapeDtypeStruct((T, D), F32),
        scratch_shapes=[pltpu.VMEM((tm, D), BF16)],
        compiler_params=_cparams(2),
        name="dense_ffn",
    )(x2, g, w_gate, w_up, w_down)


def _router_body(x_ref, g_ref, rwt_ref, e_ref, w_ref):
    h = _rms(x_ref[...], g_ref[...]).astype(BF16)
    lt = _dot_nt(rwt_ref[...], h)
    idx = lax.broadcasted_iota(jnp.int32, lt.shape, 0)
    m1 = jnp.max(lt, axis=0, keepdims=True)
    i1 = jnp.min(jnp.where(lt == m1, idx, N_EXPERTS), axis=0, keepdims=True)
    lt2 = jnp.where(idx == i1, -jnp.inf, lt)
    m2 = jnp.max(lt2, axis=0, keepdims=True)
    i2 = jnp.min(jnp.where(lt2 == m2, idx, N_EXPERTS), axis=0, keepdims=True)
    e2 = jnp.exp(m2 - m1)
    den = 1.0 + e2
    e_ref[...] = jnp.concatenate([i1, i2], axis=0)
    w_ref[...] = jnp.concatenate([1.0 / den, e2 / den], axis=0)


def moe_router(x2, g, rw_t, tm):
    T, D = x2.shape
    return pl.pallas_call(
        _router_body,
        grid=(T // tm,),
        in_specs=[pl.BlockSpec((tm, D), lambda i: (i, 0)),
                  pl.BlockSpec((1, D), lambda i: (0, 0)),
                  pl.BlockSpec((N_EXPERTS, D), lambda i: (0, 0))],
        out_specs=[pl.BlockSpec((TOP_K, tm), lambda i: (0, i)),
                   pl.BlockSpec((TOP_K, tm), lambda i: (0, i))],
        out_shape=[jax.ShapeDtypeStruct((TOP_K, T), jnp.int32),
                   jax.ShapeDtypeStruct((TOP_K, T), F32)],
        compiler_params=_cparams(1),
        name="moe_router",
    )(x2, g, rw_t)


def _moe_body(be_ref, nv_ref, nu_ref, tok_ref, x_hbm, g_ref, wg_ref, wu_ref, wd_ref, o_ref,
              xbuf, h_ref, sem, *, tm):
    i = pl.program_id(0)
    f = pl.program_id(1)
    n_used = nu_ref[0]
    hm = tm // 2

    def row_copy(idx, r):
        return pltpu.make_async_copy(x_hbm.at[pl.ds(idx, 1), :], xbuf.at[pl.ds(r, 1), :], sem.at[0])

    def gather(blk):
        def issue(r8, c):
            for u in range(DMA_UNROLL):
                r = r8 * DMA_UNROLL + u
                row_copy(tok_ref[blk * tm + r], r).start()
            return c
        n_rows = jnp.where(nv_ref[blk] > hm, tm, hm)
        lax.fori_loop(0, n_rows // DMA_UNROLL, issue, 0)

    def wait_rows(n_rows):
        for r in range(n_rows):
            row_copy(0, r).wait()

    def compute(rows):
        h = h_ref[0:rows, :]
        gate = _dot(h, wg_ref[...].astype(BF16))
        up = _dot(h, wu_ref[...].astype(BF16))
        mid = (_silu(gate) * up).astype(BF16)
        o_ref[0:rows, :] += _dot(mid, wd_ref[...].astype(BF16))

    active = i < n_used
    full = nv_ref[i] > hm

    @pl.when((f == 0) & (i == 0))
    def _():
        gather(0)

    @pl.when((f == 0) & active & full)
    def _():
        wait_rows(tm)
        h_ref[...] = _rms(xbuf[...], g_ref[...]).astype(BF16)

    @pl.when((f == 0) & active & jnp.logical_not(full))
    def _():
        wait_rows(hm)
        h_ref[0:hm, :] = _rms(xbuf[0:hm, :], g_ref[...]).astype(BF16)

    @pl.when((f == 1) & (i + 1 < n_used))
    def _():
        gather(i + 1)

    @pl.when(f == 0)
    def _():
        o_ref[...] = jnp.zeros_like(o_ref)

    @pl.when(active & full)
    def _():
        compute(tm)

    @pl.when(active & jnp.logical_not(full))
    def _():
        compute(hm)


def moe_experts(x2, g, w_gate, w_up, w_down, block_e, n_valid, n_used, row_tok, tm, tf):
    T, D = x2.shape
    E, _, F = w_gate.shape
    n_rows = row_tok.shape[0]
    nf = F // tf
    assert nf >= 2

    def fidx(i, f, nu):
        return jnp.where(i < nu[0], f, nf - 1)

    body = functools.partial(_moe_body, tm=tm)
    return pl.pallas_call(
        body,
        grid_spec=pltpu.PrefetchScalarGridSpec(
            num_scalar_prefetch=4,
            grid=(n_rows // tm, nf),
            in_specs=[pl.BlockSpec(memory_space=pl.ANY),
                      pl.BlockSpec((1, D), lambda i, f, be, nv, nu, tok: (0, 0)),
                      pl.BlockSpec((None, D, tf), lambda i, f, be, nv, nu, tok: (be[i], 0, fidx(i, f, nu))),
                      pl.BlockSpec((None, D, tf), lambda i, f, be, nv, nu, tok: (be[i], 0, fidx(i, f, nu))),
                      pl.BlockSpec((None, tf, D), lambda i, f, be, nv, nu, tok: (be[i], fidx(i, f, nu), 0))],
            out_specs=pl.BlockSpec((tm, D), lambda i, f, be, nv, nu, tok: (i, 0)),
            scratch_shapes=[pltpu.VMEM((tm, D), F32),
                            pltpu.VMEM((tm, D), BF16),
                            pltpu.SemaphoreType.DMA((1,))]),
        out_shape=jax.ShapeDtypeStruct((n_rows, D), F32),
        compiler_params=_cparams(2),
        name="moe_experts",
    )(block_e, n_valid, n_used, row_tok, x2, g, w_gate, w_up, w_down)


def _combine_body(dest_ref, yg_hbm, x_ref, tw_ref, g_ref, o_ref, ybuf, sem, *, tt, n_steps, final_norm):
    i = pl.program_id(0)

    def gather(blk, slot):
        def issue(r8, c):
            for u in range(DMA_UNROLL):
                r = r8 * DMA_UNROLL + u
                for k in range(TOP_K):
                    pltpu.make_async_copy(yg_hbm.at[pl.ds(dest_ref[(blk * tt + r) * TOP_K + k], 1), :],
                                          ybuf.at[slot, k, pl.ds(r, 1), :], sem.at[slot]).start()
            return c
        lax.fori_loop(0, tt // DMA_UNROLL, issue, 0)

    @pl.when(i == 0)
    def _():
        gather(0, 0)

    slot = i % 2

    @pl.when(i + 1 < n_steps)
    def _():
        gather(i + 1, 1 - slot)

    for r in range(tt):
        for k in range(TOP_K):
            pltpu.make_async_copy(yg_hbm.at[pl.ds(0, 1), :], ybuf.at[slot, k, pl.ds(r, 1), :],
                                  sem.at[slot]).wait()

    tw = tw_ref[...]
    y = x_ref[...] + ybuf[slot, 0] * tw[:, 0:1] + ybuf[slot, 1] * tw[:, 1:2]
    if final_norm:
        y = _rms(y, g_ref[...])
    o_ref[...] = y


def moe_combine(yg, x2, tw_col, dest, g_final, tt, final_norm):
    T, D = x2.shape
    n_steps = T // tt
    body = functools.partial(_combine_body, tt=tt, n_steps=n_steps, final_norm=final_norm)
    return pl.pallas_call(
        body,
        grid_spec=pltpu.PrefetchScalarGridSpec(
            num_scalar_prefetch=1,
            grid=(n_steps,),
            in_specs=[pl.BlockSpec(memory_space=pl.ANY),
                      pl.BlockSpec((tt, D), lambda i, d: (i, 0)),
                      pl.BlockSpec((tt, LANES), lambda i, d: (i, 0)),
                      pl.BlockSpec((1, D), lambda i, d: (0, 0))],
            out_specs=pl.BlockSpec((tt, D), lambda i, d: (i, 0)),
            scratch_shapes=[pltpu.VMEM((2, TOP_K, tt, D), F32),
                            pltpu.SemaphoreType.DMA((2,))]),
        out_shape=jax.ShapeDtypeStruct((T, D), F32),
        compiler_params=_cparams(1),
        name="moe_combine",
    )(dest, yg, x2, tw_col, g_final)


def moe_plan(top_e, tm):
    T = top_e.shape[1]
    TK = T * TOP_K
    experts = jnp.arange(N_EXPERTS, dtype=jnp.int32)
    rows = TK // LANES
    onehot = (top_e.T.reshape(1, rows, LANES) == experts[:, None, None]).astype(F32)
    tri = (jnp.arange(LANES)[:, None] <= jnp.arange(LANES)[None, :]).astype(F32)
    within = jnp.einsum('erk,kl->erl', onehot, tri, precision=lax.Precision.HIGHEST)
    before = (jnp.arange(rows)[:, None] < jnp.arange(rows)[None, :]).astype(F32)
    row_off = jnp.einsum('er,rs->es', within[:, :, -1], before, precision=lax.Precision.HIGHEST)
    cum = within + row_off[:, :, None]
    counts = cum[:, -1, -1].astype(jnp.int32)
    padded = (counts + tm - 1) // tm * tm
    pad_end = jnp.cumsum(padded)
    pad_start = pad_end - padded
    dest = jnp.sum(onehot * (cum - 1.0 + pad_start.astype(F32)[:, None, None]), axis=0)
    dest = dest.reshape(TK).astype(jnp.int32)
    n_rows = TK + N_EXPERTS * tm
    n_blocks = n_rows // tm
    row_tok = jnp.zeros((n_rows,), jnp.int32).at[dest].set(jnp.arange(TK, dtype=jnp.int32) // TOP_K)
    blk = jnp.arange(n_blocks, dtype=jnp.int32)
    blk_onehot = ((blk[:, None] * tm >= pad_start[None, :]) & (blk[:, None] * tm < pad_end[None, :])).astype(jnp.int32)
    n_used = (pad_end[-1:] // tm).astype(jnp.int32)
    last_e = jnp.max(jnp.where(padded > 0, experts, 0))
    block_e = jnp.where(blk < n_used[0], jnp.sum(blk_onehot * experts[None, :], axis=1), last_e)
    n_valid = jnp.clip(jnp.sum(blk_onehot * (counts + pad_start)[None, :], axis=1) - blk * tm, 0, tm)
    return dest.astype(jnp.int32), row_tok, block_e.astype(jnp.int32), n_valid.astype(jnp.int32), n_used


def moe_block(x2, g, router_w, w_gate, w_up, w_down, g_final, final_norm, tm_r, tm_e, tf, tt):
    top_e, top_w = moe_router(x2, g, router_w.T.astype(BF16), tm_r)
    dest, row_tok, block_e, n_valid, n_used = moe_plan(top_e, tm_e)
    yg = moe_experts(x2, g, w_gate, w_up, w_down, block_e, n_valid, n_used, row_tok, tm_e, tf)
    tw_col = jnp.pad(top_w.T, ((0, 0), (0, LANES - TOP_K)))
    return moe_combine(yg, x2, tw_col, dest, g_final, tt, final_norm)


def _norm_body(x_ref, g_ref, o_ref):
    o_ref[...] = _rms(x_ref[...], g_ref[...])


def final_norm(x2, g, tm):
    T, D = x2.shape
    return pl.pallas_call(
        _norm_body,
        grid=(T // tm,),
        in_specs=[pl.BlockSpec((tm, D), lambda i: (i, 0)), pl.BlockSpec((1, D), lambda i: (0, 0))],
        out_specs=pl.BlockSpec((tm, D), lambda i: (i, 0)),
        out_shape=jax.ShapeDtypeStruct((T, D), F32),
        compiler_params=_cparams(1),
        name="final_norm",
    )(x2, g)


def _split_w_in(w):
    lo, hi = COL_SQ, COL_SQ + BA_COLS
    w_main = jnp.concatenate([w[:, :, :lo], w[:, :, hi:]], axis=2).astype(BF16)
    w_ba = jnp.pad(w[:, :, lo:hi], ((0, 0), (0, 0), (0, LANES - BA_COLS))).astype(BF16)
    return w_main, w_ba


def _pad_lanes(vec, offset):
    return jnp.pad(vec.astype(F32), (offset, LANES - offset - vec.shape[0])).reshape(1, LANES)


def mixer_layer(x2, batch, seq, layer, ln_g, w_main, w_ba, conv_w, a_log, dt_bias, gdn_norm_g, bias_tbl,
                sink_tbl, w_proj_gdn, w_proj_swa, w_out):
    proj, ba = in_proj(x2, ln_g.reshape(1, -1), w_main, w_ba, layer, tm=1024, tn=1280)
    qn, kn, vv, gb = gdn_prep(proj, ba, conv_w, _pad_lanes(a_log, GDN_V_HEADS),
                              _pad_lanes(dt_bias, GDN_V_HEADS), seq, ts=256)
    n_chunks = seq // GDN_CHUNK
    gcr = gb[:, GDN_V_HEADS:2 * GDN_V_HEADS].reshape(batch, n_chunks, GDN_CHUNK, GDN_V_HEADS)
    gcr = gcr.transpose(0, 3, 1, 2)
    o_a = gdn_core(qn, kn, vv, proj, gb, gcr, gdn_norm_g.reshape(1, -1).astype(F32), batch, seq, sb=512)
    o_b = swa_core(proj, bias_tbl, sink_tbl, batch, seq)
    y = merge_proj(o_a, o_b, w_proj_gdn, w_proj_swa, proj, layer, tm=1024, tn=512)
    return out_proj(y, w_out, x2, layer, tm=1024, tn=512)


def kernel(x, ln_mix_g, w_in, conv_w, a_log, dt_bias, gdn_norm_g, sinks, rel_bias, w_proj_gdn, w_proj_swa,
           w_out, ln_ffn_g, ffn_w_gate, ffn_w_up, ffn_w_down, router_w, moe_w_gate, moe_w_up, moe_w_down,
           ln_final_g):
    B, S, D = x.shape
    depth = w_in.shape[0]
    x2 = x.reshape(B * S, D)
    g_final = ln_final_g.reshape(1, D)
    w_main, w_ba = _split_w_in(w_in)
    w_proj_gdn, w_proj_swa, w_out = w_proj_gdn.astype(BF16), w_proj_swa.astype(BF16), w_out.astype(BF16)
    normed = False
    for i in range(depth):
        bias_tbl, sink_tbl = swa_tables(rel_bias, sinks[i])
        x2 = mixer_layer(x2, B, S, i, ln_mix_g[i], w_main, w_ba, conv_w[i], a_log[i], dt_bias[i],
                         gdn_norm_g[i], bias_tbl, sink_tbl, w_proj_gdn, w_proj_swa, w_out)
        g_ffn = ln_ffn_g[i].reshape(1, D)
        j = i // 2
        if i % 2 == 0:
            f = ffn_w_gate.shape[2]
            fpad = (-f) % 512
            wg = jnp.pad(ffn_w_gate[j], ((0, 0), (0, fpad))).astype(BF16)
            wu = jnp.pad(ffn_w_up[j], ((0, 0), (0, fpad))).astype(BF16)
            wd = jnp.pad(ffn_w_down[j], ((0, fpad), (0, 0))).astype(BF16)
            x2 = dense_ffn(x2, g_ffn, wg, wu, wd, tm=1024, tf=512)
        else:
            last = i == depth - 1
            x2 = moe_block(x2, g_ffn, router_w[j], moe_w_gate[j], moe_w_up[j], moe_w_down[j],
                           g_final, last, tm_r=512, tm_e=1024, tf=256, tt=256)
            normed = last
    if not normed:
        x2 = final_norm(x2, g_final, tm=512)
    return x2.reshape(B, S, D)
```

```python
import functools
import math

import jax
import jax.numpy as jnp
from jax import lax
from jax.experimental import pallas as pl
from jax.experimental.pallas import tpu as pltpu

F32 = jnp.float32
BF16 = jnp.bfloat16

D_MODEL = 2048
GDN_QK_HEADS = 8
GDN_V_HEADS = 16
GDN_HEAD_DIM = 128
GDN_CONV = 4
GDN_CHUNK = 64
SWA_Q_HEADS = 32
SWA_KV_HEADS = 4
SWA_HEAD_DIM = 64
SWA_WINDOW = 128
REL_BUCKETS = 32
REL_MAX_DIST = 128
N_EXPERTS = 8
TOP_K = 2
NORM_EPS = 1e-6
GDN_KEY_DIM = GDN_QK_HEADS * GDN_HEAD_DIM
GDN_VAL_DIM = GDN_V_HEADS * GDN_HEAD_DIM
SWA_Q_DIM = SWA_Q_HEADS * SWA_HEAD_DIM
SWA_KV_DIM = SWA_KV_HEADS * SWA_HEAD_DIM

LANES = 128
VMEM_LIMIT = 56 * 1024 * 1024

COL_Q = 0
COL_K = COL_Q + GDN_KEY_DIM
COL_V = COL_K + GDN_KEY_DIM
COL_Z = COL_V + GDN_VAL_DIM
COL_SQ = COL_Z + GDN_VAL_DIM
COL_SK = COL_SQ + SWA_Q_DIM
COL_SV = COL_SK + SWA_KV_DIM
COL_GA = COL_SV + SWA_KV_DIM
COL_GB = COL_GA + D_MODEL
PROJ_COLS = COL_GB + D_MODEL
BA_COLS = 2 * GDN_V_HEADS
GDN_CONV_DIM = 2 * GDN_KEY_DIM + GDN_VAL_DIM

MASKED = -1e30
DMA_UNROLL = 8


def _cparams(n_axes):
    return pltpu.CompilerParams(dimension_semantics=("arbitrary",) * n_axes,
                                vmem_limit_bytes=VMEM_LIMIT)


def _rms(x, g):
    return x * lax.rsqrt(jnp.mean(x * x, axis=-1, keepdims=True) + NORM_EPS) * g


def _silu(x):
    return x * jax.nn.sigmoid(x)


def _dot(a, b):
    return jnp.dot(a, b, preferred_element_type=F32)


def _dot_nt(a, b):
    return lax.dot_general(a, b, (((1,), (1,)), ((), ())), preferred_element_type=F32)


def _dot_tn(a, b):
    return lax.dot_general(a, b, (((0,), (0,)), ((), ())), preferred_element_type=F32)


def _inproj_body(x_ref, g_ref, w_ref, wba_ref, o_ref, oba_ref, h_ref):
    @pl.when(pl.program_id(1) == 0)
    def _():
        h = _rms(x_ref[...], g_ref[...]).astype(BF16)
        h_ref[...] = h
        oba_ref[...] = _dot(h, wba_ref[...])

    o_ref[...] = _dot(h_ref[...], w_ref[...]).astype(o_ref.dtype)


def in_proj(x2, g, w_main, w_ba, layer, tm, tn):
    T, D = x2.shape
    N = w_main.shape[2]
    return pl.pallas_call(
        _inproj_body,
        grid=(T // tm, N // tn),
        in_specs=[pl.BlockSpec((tm, D), lambda i, j: (i, 0)),
                  pl.BlockSpec((1, D), lambda i, j: (0, 0)),
                  pl.BlockSpec((None, D, tn), lambda i, j: (layer, 0, j)),
                  pl.BlockSpec((None, D, LANES), lambda i, j: (layer, 0, 0))],
        out_specs=[pl.BlockSpec((tm, tn), lambda i, j: (i, j)),
                   pl.BlockSpec((tm, LANES), lambda i, j: (i, 0))],
        out_shape=[jax.ShapeDtypeStruct((T, N), BF16), jax.ShapeDtypeStruct((T, LANES), F32)],
        scratch_shapes=[pltpu.VMEM((tm, D), BF16)],
        compiler_params=_cparams(2),
        name="in_proj",
    )(x2, g, w_main, w_ba)


def _prep_body(x_ref, halo_ref, cw_ref, ba_ref, alog_ref, dtb_ref, q_ref, k_ref, v_ref, gb_ref,
               *, ts, seq, n_qk, n_v, chunk):
    first = (pl.program_id(0) * ts) % seq == 0
    for c in range(2 * n_qk + n_v):
        cs = slice(c * LANES, (c + 1) * LANES)
        cur = x_ref[:, cs].astype(F32)
        hal = halo_ref[8:16, cs].astype(F32)
        hal = jnp.where(first, 0.0, hal)
        ext = jnp.concatenate([hal, cur], axis=0)
        y = ext[5:5 + ts] * cw_ref[0:1, cs]
        y = y + ext[6:6 + ts] * cw_ref[1:2, cs]
        y = y + ext[7:7 + ts] * cw_ref[2:3, cs]
        y = y + cur * cw_ref[3:4, cs]
        y = _silu(y)
        if c < 2 * n_qk:
            y = y * lax.rsqrt(jnp.sum(y * y, axis=-1, keepdims=True) + 1e-6)
        if c < n_qk:
            q_ref[:, cs] = (y * (GDN_HEAD_DIM ** -0.5)).astype(q_ref.dtype)
        elif c < 2 * n_qk:
            k_ref[:, (c - n_qk) * LANES:(c - n_qk + 1) * LANES] = y.astype(k_ref.dtype)
        else:
            v_ref[:, (c - 2 * n_qk) * LANES:(c - 2 * n_qk + 1) * LANES] = y.astype(v_ref.dtype)

    ba = ba_ref[...]
    lane = lax.broadcasted_iota(jnp.int32, ba.shape, 1)
    pos = lax.broadcasted_iota(jnp.int32, ba.shape, 0) % chunk
    beta = jax.nn.sigmoid(ba)
    t = ba + dtb_ref[...]
    softplus = jnp.maximum(t, 0.0) + jnp.log1p(jnp.exp(-jnp.abs(t)))
    gc = -jnp.exp(alog_ref[...]) * softplus
    s = 1
    while s < chunk:
        gc = gc + jnp.where(pos >= s, pltpu.roll(gc, s, 0), 0.0)
        s *= 2
    gb_ref[...] = jnp.where(lane < n_v, beta, jnp.where(lane < 2 * n_v, gc, 0.0))


def gdn_prep(proj, ba, conv_w, alog_pad, dtb_pad, seq, ts):
    T = proj.shape[0]
    n_qk, n_v = GDN_QK_HEADS, GDN_V_HEADS
    body = functools.partial(_prep_body, ts=ts, seq=seq, n_qk=n_qk, n_v=n_v, chunk=GDN_CHUNK)
    return pl.pallas_call(
        body,
        grid=(T // ts,),
        in_specs=[pl.BlockSpec((ts, GDN_CONV_DIM), lambda i: (i, 0)),
                  pl.BlockSpec((16, GDN_CONV_DIM), lambda i: (jnp.maximum(i * (ts // 16) - 1, 0), 0)),
                  pl.BlockSpec((GDN_CONV, GDN_CONV_DIM), lambda i: (0, 0)),
                  pl.BlockSpec((ts, LANES), lambda i: (i, 0)),
                  pl.BlockSpec((1, LANES), lambda i: (0, 0)),
                  pl.BlockSpec((1, LANES), lambda i: (0, 0))],
        out_specs=[pl.BlockSpec((ts, GDN_KEY_DIM), lambda i: (i, 0)),
                   pl.BlockSpec((ts, GDN_KEY_DIM), lambda i: (i, 0)),
                   pl.BlockSpec((ts, GDN_VAL_DIM), lambda i: (i, 0)),
                   pl.BlockSpec((ts, LANES), lambda i: (i, 0))],
        out_shape=[jax.ShapeDtypeStruct((T, GDN_KEY_DIM), BF16),
                   jax.ShapeDtypeStruct((T, GDN_KEY_DIM), BF16),
                   jax.ShapeDtypeStruct((T, GDN_VAL_DIM), BF16),
                   jax.ShapeDtypeStruct((T, LANES), F32)],
        compiler_params=_cparams(1),
        name="gdn_prep",
    )(proj, proj, conv_w, ba, alog_pad, dtb_pad)


def _gdn_body(q_ref, k_ref, v_ref, z_ref, gb_ref, gcr_ref, gn_ref, o_ref,
              u_s, wq_s, qkd_s, gl_s, st_s, *, n_chunks, n_qk, n_v):
    C = GDN_CHUNK
    hd = GDN_HEAD_DIM
    rep = n_v // n_qk

    @pl.when(pl.program_id(1) == 0)
    def _():
        st_s[...] = jnp.zeros_like(st_s)

    row = lax.broadcasted_iota(jnp.int32, (C, C), 0)
    col = lax.broadcasted_iota(jnp.int32, (C, C), 1)
    causal = row >= col
    strict = row > col

    hs = range(n_qk)
    hvs = range(n_v)

    def phase1(c, carry):
        rows = pl.ds(pl.multiple_of(c * C, C), C)
        rows2 = pl.ds(pl.multiple_of(c * 2 * C, 2 * C), C)
        rows2b = pl.ds(pl.multiple_of(c * 2 * C, 2 * C) + C, C)
        gbc = gb_ref[rows, :]
        q = [q_ref[rows, h * hd:(h + 1) * hd] for h in hs]
        k = [k_ref[rows, h * hd:(h + 1) * hd] for h in hs]
        kk = [_dot_nt(k[h], k[h]) for h in hs]
        qk = [_dot_nt(q[h], k[h]) for h in hs]
        bcol = [gbc[:, hv:hv + 1] for hv in hvs]
        gcol = [gbc[:, n_v + hv:n_v + hv + 1] for hv in hvs]
        glast = [gbc[C - 1:C, n_v + hv:n_v + hv + 1] for hv in hvs]
        decay = [jnp.exp(jnp.where(causal, gcol[hv] - gcr_ref[0, hv, pl.ds(c, 1), :], -jnp.inf)) for hv in hvs]
        bk = [jnp.where(strict, kk[hv // rep] * bcol[hv] * decay[hv], 0.0) * -1.0 for hv in hvs]
        nn = list(bk)
        egc = None
        p = 1
        while 2 * p < C:
            bb = [x.astype(BF16) for x in bk]
            sq = [_dot(x, x) for x in bb]
            if egc is not None:
                nn = [n + b + _dot(n.astype(BF16), b.astype(BF16)) for n, b in zip(nn, bk)]
            bk = sq
            if egc is None:
                egc = [jnp.exp(g) for g in gcol]
                for hv in hvs:
                    qf = q[hv // rep].astype(F32)
                    kf = k[hv // rep].astype(F32)
                    wq_s[hv, rows2b, :] = (qf * egc[hv]).astype(BF16)
                    kd = kf * jnp.exp(glast[hv] - gcol[hv])
                    qkd_s[hv, c, 0:C, :] = (qk[hv // rep] * decay[hv]).astype(BF16)
                    qkd_s[hv, c, C:3 * C, :] = kd.T.astype(BF16)
                    gl_s[hv, pl.ds(c, 1), :] = jnp.broadcast_to(jnp.exp(glast[hv]), (1, LANES))
            p *= 2
        nn = [n + b + _dot(n.astype(BF16), b.astype(BF16)) for n, b in zip(nn, bk)]
        nb = [n.astype(BF16) for n in nn]
        vb = [v_ref[rows, hv * hd:(hv + 1) * hd].astype(F32) * bcol[hv] for hv in hvs]
        kbg = [k[hv // rep].astype(F32) * (bcol[hv] * egc[hv]) for hv in hvs]
        uu = [_dot(nb[hv], vb[hv].astype(BF16)) for hv in hvs]
        ww = [_dot(nb[hv], kbg[hv].astype(BF16)) for hv in hvs]
        for hv in hvs:
            u_s[hv, rows, :] = vb[hv] + uu[hv]
            wq_s[hv, rows2, :] = (kbg[hv] + ww[hv]).astype(BF16)
        return carry

    lax.fori_loop(0, n_chunks, phase1, 0)

    def phase2(c, carry):
        rows = pl.ds(pl.multiple_of(c * C, C), C)
        rows_wq = pl.ds(pl.multiple_of(c * 2 * C, 2 * C), 2 * C)
        sb = [st_s[hv].astype(BF16) for hv in hvs]
        ws = [_dot(wq_s[hv, rows_wq, :], sb[hv]) for hv in hvs]
        vnb = [(u_s[hv, rows, :] - ws[hv][0:C]).astype(BF16) for hv in hvs]
        ov = [_dot(qkd_s[hv, c], vnb[hv]) for hv in hvs]
        for hv in hvs:
            st_s[hv] = st_s[hv] * gl_s[hv, pl.ds(c, 1), :] + ov[hv][C:3 * C]
        for hv in hvs:
            o = ws[hv][C:2 * C] + ov[hv][0:C]
            o = o * lax.rsqrt(jnp.mean(o * o, axis=-1, keepdims=True) + NORM_EPS) * gn_ref[...]
            o = o * _silu(z_ref[rows, hv * hd:(hv + 1) * hd].astype(F32))
            o_ref[rows, hv * hd:(hv + 1) * hd] = o.astype(o_ref.dtype)
        return carry

    lax.fori_loop(0, n_chunks, phase2, 0)


def gdn_core(qn, kn, vv, proj, gb, gcr, gnorm, batch, seq, sb):
    T = qn.shape[0]
    n_qk, n_v, hd, C = GDN_QK_HEADS, GDN_V_HEADS, GDN_HEAD_DIM, GDN_CHUNK
    nc = sb // C
    nsb = seq // sb
    zc = COL_Z // (n_v * hd)
    body = functools.partial(_gdn_body, n_chunks=nc, n_qk=n_qk, n_v=n_v)
    blk = lambda b, s: (b * nsb + s, 0)
    return pl.pallas_call(
        body,
        grid=(batch, nsb),
        in_specs=[pl.BlockSpec((sb, n_qk * hd), blk),
                  pl.BlockSpec((sb, n_qk * hd), blk),
                  pl.BlockSpec((sb, n_v * hd), blk),
                  pl.BlockSpec((sb, n_v * hd), lambda b, s: (b * nsb + s, zc)),
                  pl.BlockSpec((sb, LANES), blk),
                  pl.BlockSpec((1, n_v, nc, C), lambda b, s: (b, 0, s, 0)),
                  pl.BlockSpec((1, hd), lambda b, s: (0, 0))],
        out_specs=pl.BlockSpec((sb, n_v * hd), blk),
        out_shape=jax.ShapeDtypeStruct((T, n_v * hd), BF16),
        scratch_shapes=[pltpu.VMEM((n_v, sb, hd), F32),
                        pltpu.VMEM((n_v, 2 * sb, hd), BF16),
                        pltpu.VMEM((n_v, nc, 3 * C, C), BF16),
                        pltpu.VMEM((n_v, nc, LANES), F32),
                        pltpu.VMEM((n_v, hd, hd), F32)],
        compiler_params=_cparams(2),
        name="gdn_core",
    )(qn, kn, vv, proj, gb, gcr, gnorm)


def _swa_body(q_ref, kp_ref, kc_ref, vp_ref, vc_ref, bias_ref, sink_ref, o_ref):
    W, dh = SWA_WINDOW, SWA_HEAD_DIM
    pairs = (SWA_Q_HEADS // SWA_KV_HEADS) // 2
    kvs = range(SWA_KV_HEADS)
    zero = jnp.zeros((2 * W, dh), BF16)
    one = jnp.ones((2 * W, dh), BF16)
    low = lax.broadcasted_iota(jnp.int32, (1, LANES), 1) < dh

    logits = []
    for hk in kvs:
        ks = slice(hk * dh, (hk + 1) * dh)
        kb = jnp.concatenate([kp_ref[:, ks], kc_ref[:, ks]], axis=0)
        krhs = jnp.concatenate([jnp.concatenate([kb, zero], axis=1),
                                jnp.concatenate([zero, kb], axis=1)], axis=0)
        qs = jnp.concatenate([q_ref[:, (hk * pairs + p) * LANES:(hk * pairs + p + 1) * LANES]
                              for p in range(pairs)], axis=0)
        qs = qs * jnp.asarray(dh ** -0.5, BF16)
        logits.append(_dot_nt(qs, krhs) + bias_ref[0, hk])

    outs, sinks = [], []
    for hk in kvs:
        ks = slice(hk * dh, (hk + 1) * dh)
        vb = jnp.concatenate([vp_ref[:, ks], vc_ref[:, ks]], axis=0)
        vrhs = jnp.concatenate([jnp.concatenate([vb, zero, one, zero], axis=1),
                                jnp.concatenate([zero, vb, zero, one], axis=1)], axis=0)
        ps, ms = [], []
        for c in range(2):
            lg = logits[hk][:, c * 2 * W:(c + 1) * 2 * W]
            sk = sink_ref[hk, c]
            m = jnp.max(jnp.maximum(jnp.maximum(lg[:, :W], lg[:, W:]), sk), axis=-1, keepdims=True)
            ps.append(jnp.exp(lg - m).astype(BF16))
            ms.append(sk - m)
        outs.append(_dot(jnp.concatenate(ps, axis=1), vrhs))
        sinks.append(jnp.exp(jnp.where(low, ms[0], ms[1])))

    for hk in kvs:
        out = outs[hk][:, :LANES] / (outs[hk][:, LANES:] + sinks[hk])
        for p in range(pairs):
            o_ref[:, (hk * pairs + p) * LANES:(hk * pairs + p + 1) * LANES] = (
                out[p * W:(p + 1) * W].astype(o_ref.dtype))


def swa_core(proj, bias_tbl, sink_tbl, batch, seq):
    T = proj.shape[0]
    W = SWA_WINDOW
    nb = seq // W
    qc, kc, vc = COL_SQ // SWA_Q_DIM, COL_SK // SWA_KV_DIM, COL_SV // SWA_KV_DIM
    cur = lambda b, n: b * nb + n
    prev = lambda b, n: jnp.maximum(b * nb + n - 1, 0)
    return pl.pallas_call(
        _swa_body,
        grid=(batch, nb),
        in_specs=[pl.BlockSpec((W, SWA_Q_DIM), lambda b, n: (cur(b, n), qc)),
                  pl.BlockSpec((W, SWA_KV_DIM), lambda b, n: (prev(b, n), kc)),
                  pl.BlockSpec((W, SWA_KV_DIM), lambda b, n: (cur(b, n), kc)),
                  pl.BlockSpec((W, SWA_KV_DIM), lambda b, n: (prev(b, n), vc)),
                  pl.BlockSpec((W, SWA_KV_DIM), lambda b, n: (cur(b, n), vc)),
                  pl.BlockSpec((1,) + bias_tbl.shape[1:], lambda b, n: (jnp.minimum(n, 1), 0, 0, 0)),
                  pl.BlockSpec(sink_tbl.shape, lambda b, n: (0, 0, 0, 0))],
        out_specs=pl.BlockSpec((W, SWA_Q_DIM), lambda b, n: (cur(b, n), 0)),
        out_shape=jax.ShapeDtypeStruct((T, SWA_Q_DIM), BF16),
        compiler_params=_cparams(2),
        name="swa_core",
    )(proj, proj, proj, proj, proj, bias_tbl, sink_tbl)


def _t5_bucket(dist):
    max_exact = REL_BUCKETS // 2
    large = max_exact + (jnp.log(jnp.maximum(dist, 1).astype(F32) / max_exact)
                         / math.log(REL_MAX_DIST / max_exact) * (REL_BUCKETS - max_exact)).astype(jnp.int32)
    large = jnp.minimum(large, REL_BUCKETS - 1)
    return jnp.where(dist < max_exact, dist, large)


def swa_tables(rel_bias, sinks):
    W, hq, hkv = SWA_WINDOW, SWA_Q_HEADS, SWA_KV_HEADS
    pairs = (hq // hkv) // 2
    by_dist = rel_bias.astype(F32)[_t5_bucket(jnp.arange(W))]
    period = 3 * W
    vec = jnp.concatenate([jnp.full((1, hq), MASKED, F32), by_dist[::-1],
                           jnp.full((period - W - 1, hq), MASKED, F32)], axis=0).T
    tbl = jnp.tile(vec, (1, W))[:, :W * (period - 1)].reshape(hq, W, period - 1)[:, :, :2 * W]
    first = jnp.where(jnp.arange(2 * W)[None, None, :] < W, MASKED, tbl)
    tbl = jnp.stack([first, tbl])
    tbl = tbl.reshape(2, hkv, pairs, 2, W, 2 * W).transpose(0, 1, 2, 4, 3, 5)
    tbl = tbl.reshape(2, hkv, pairs * W, 4 * W)
    sk = sinks.astype(F32).reshape(hkv, pairs, 2).transpose(0, 2, 1)
    sk = jnp.broadcast_to(sk[:, :, :, None, None], (hkv, 2, pairs, W, LANES)).reshape(hkv, 2, pairs * W, LANES)
    return tbl, sk


def _merge_body(oa_ref, ob_ref, wa_ref, wb_ref, ga_ref, gb_ref, y_ref):
    a = _dot(oa_ref[...], wa_ref[...])
    b = _dot(ob_ref[...], wb_ref[...])
    y = jax.nn.sigmoid(ga_ref[...].astype(F32)) * a + jax.nn.sigmoid(gb_ref[...].astype(F32)) * b
    y_ref[...] = y.astype(y_ref.dtype)


def merge_proj(o_a, o_b, w_a, w_b, proj, layer, tm, tn):
    T, K = o_a.shape
    N = w_a.shape[2]
    ga0, gb0 = COL_GA // tn, COL_GB // tn
    return pl.pallas_call(
        _merge_body,
        grid=(T // tm, N // tn),
        in_specs=[pl.BlockSpec((tm, K), lambda i, j: (i, 0)),
                  pl.BlockSpec((tm, K), lambda i, j: (i, 0)),
                  pl.BlockSpec((None, K, tn), lambda i, j: (layer, 0, j)),
                  pl.BlockSpec((None, K, tn), lambda i, j: (layer, 0, j)),
                  pl.BlockSpec((tm, tn), lambda i, j: (i, ga0 + j)),
                  pl.BlockSpec((tm, tn), lambda i, j: (i, gb0 + j))],
        out_specs=pl.BlockSpec((tm, tn), lambda i, j: (i, j)),
        out_shape=jax.ShapeDtypeStruct((T, N), BF16),
        compiler_params=_cparams(2),
        name="merge_proj",
    )(o_a, o_b, w_a, w_b, proj, proj)


def _resid_body(y_ref, w_ref, x_ref, o_ref):
    o_ref[...] = x_ref[...] + _dot(y_ref[...], w_ref[...])


def out_proj(y, w, x2, layer, tm, tn):
    T, K = y.shape
    N = w.shape[2]
    return pl.pallas_call(
        _resid_body,
        grid=(T // tm, N // tn),
        in_specs=[pl.BlockSpec((tm, K), lambda i, j: (i, 0)),
                  pl.BlockSpec((None, K, tn), lambda i, j: (layer, 0, j)),
                  pl.BlockSpec((tm, tn), lambda i, j: (i, j))],
        out_specs=pl.BlockSpec((tm, tn), lambda i, j: (i, j)),
        out_shape=jax.ShapeDtypeStruct((T, N), F32),
        compiler_params=_cparams(2),
        name="out_proj",
    )(y, w, x2)


def _ffn_body(x_ref, g_ref, wg_ref, wu_ref, wd_ref, o_ref, h_ref):
    @pl.when(pl.program_id(1) == 0)
    def _():
        x = x_ref[...]
        h_ref[...] = _rms(x, g_ref[...]).astype(BF16)
        o_ref[...] = x

    h = h_ref[...]
    mid = (_silu(_dot(h, wg_ref[...])) * _dot(h, wu_ref[...])).astype(BF16)
    o_ref[...] += _dot(mid, wd_ref[...])


def dense_ffn(x2, g, w_gate, w_up, w_down, tm, tf):
    T, D = x2.shape
    F = w_gate.shape[1]
    return pl.pallas_call(
        _ffn_body,
        grid=(T // tm, F // tf),
        in_specs=[pl.BlockSpec((tm, D), lambda i, f: (i, 0)),
                  pl.BlockSpec((1, D), lambda i, f: (0, 0)),
                  pl.BlockSpec((D, tf), lambda i, f: (0, f)),
                  pl.BlockSpec((D, tf), lambda i, f: (0, f)),
                  pl.BlockSpec((tf, D), lambda i, f: (f, 0))],
        out_specs=pl.BlockSpec((tm, D), lambda i, f: (i, 0)),
        out_shape=jax.ShapeDtypeStruct((T, D), F32),
        scratch_shapes=[pltpu.VMEM((tm, D), BF16)],
        compiler_params=_cparams(2),
        name="dense_ffn",
    )(x2, g, w_gate, w_up, w_down)


def _router_body(x_ref, g_ref, rwt_ref, e_ref, w_ref):
    h = _rms(x_ref[...], g_ref[...]).astype(BF16)
    lt = _dot_nt(rwt_ref[...], h)
    idx = lax.broadcasted_iota(jnp.int32, lt.shape, 0)
    m1 = jnp.max(lt, axis=0, keepdims=True)
    i1 = jnp.min(jnp.where(lt == m1, idx, N_EXPERTS), axis=0, keepdims=True)
    lt2 = jnp.where(idx == i1, -jnp.inf, lt)
    m2 = jnp.max(lt2, axis=0, keepdims=True)
    i2 = jnp.min(jnp.where(lt2 == m2, idx, N_EXPERTS), axis=0, keepdims=True)
    e2 = jnp.exp(m2 - m1)
    den = 1.0 + e2
    e_ref[...] = jnp.concatenate([i1, i2], axis=0)
    w_ref[...] = jnp.concatenate([1.0 / den, e2 / den], axis=0)


def moe_router(x2, g, rw_t, tm):
    T, D = x2.shape
    return pl.pallas_call(
        _router_body,
        grid=(T // tm,),
        in_specs=[pl.BlockSpec((tm, D), lambda i: (i, 0)),
                  pl.BlockSpec((1, D), lambda i: (0, 0)),
                  pl.BlockSpec((N_EXPERTS, D), lambda i: (0, 0))],
        out_specs=[pl.BlockSpec((TOP_K, tm), lambda i: (0, i)),
                   pl.BlockSpec((TOP_K, tm), lambda i: (0, i))],
        out_shape=[jax.ShapeDtypeStruct((TOP_K, T), jnp.int32),
                   jax.ShapeDtypeStruct((TOP_K, T), F32)],
        compiler_params=_cparams(1),
        name="moe_router",
    )(x2, g, rw_t)


def _moe_body(be_ref, nv_ref, nu_ref, tok_ref, x_hbm, g_ref, wg_ref, wu_ref, wd_ref, o_ref,
              xbuf, h_ref, sem, *, tm, part):
    i = pl.program_id(0)
    f = pl.program_id(1)
    n_used = nu_ref[0]

    def row_copy(idx, r):
        return pltpu.make_async_copy(x_hbm.at[pl.ds(idx, 1), :], xbuf.at[pl.ds(r, 1), :], sem.at[0])

    def gather(blk):
        def issue(r8, c):
            for u in range(DMA_UNROLL):
                r = r8 * DMA_UNROLL + u
                row_copy(tok_ref[blk * tm + r], r).start()
            return c
        lax.fori_loop(0, nv_ref[blk] * (part // DMA_UNROLL), issue, 0)

    def load_rows(rows):
        for r in range(rows):
            row_copy(0, r).wait()
        h_ref[0:rows, :] = _rms(xbuf[0:rows, :], g_ref[...]).astype(BF16)

    def compute(rows):
        h = h_ref[0:rows, :]
        gate = _dot(h, wg_ref[...].astype(BF16))
        up = _dot(h, wu_ref[...].astype(BF16))
        mid = (_silu(gate) * up).astype(BF16)
        o_ref[0:rows, :] += _dot(mid, wd_ref[...].astype(BF16))

    active = i < n_used
    pieces = nv_ref[i]

    @pl.when((f == 0) & (i == 0))
    def _():
        gather(0)

    for p in range(1, tm // part + 1):
        pl.when((f == 0) & active & (pieces == p))(functools.partial(load_rows, p * part))

    @pl.when((f == 1) & (i + 1 < n_used))
    def _():
        gather(i + 1)

    @pl.when(f == 0)
    def _():
        o_ref[...] = jnp.zeros_like(o_ref)

    for p in range(1, tm // part + 1):
        pl.when(active & (pieces == p))(functools.partial(compute, p * part))


def moe_experts(x2, g, w_gate, w_up, w_down, block_e, n_valid, n_used, row_tok, tm, part, tf):
    T, D = x2.shape
    E, _, F = w_gate.shape
    n_rows = row_tok.shape[0]
    nf = F // tf
    assert nf >= 2 and tm % part == 0 and part % DMA_UNROLL == 0

    def fidx(i, f, nu):
        return jnp.where(i < nu[0], f, nf - 1)

    body = functools.partial(_moe_body, tm=tm, part=part)
    return pl.pallas_call(
        body,
        grid_spec=pltpu.PrefetchScalarGridSpec(
            num_scalar_prefetch=4,
            grid=(n_rows // tm, nf),
            in_specs=[pl.BlockSpec(memory_space=pl.ANY),
                      pl.BlockSpec((1, D), lambda i, f, be, nv, nu, tok: (0, 0)),
                      pl.BlockSpec((None, D, tf), lambda i, f, be, nv, nu, tok: (be[i], 0, fidx(i, f, nu))),
                      pl.BlockSpec((None, D, tf), lambda i, f, be, nv, nu, tok: (be[i], 0, fidx(i, f, nu))),
                      pl.BlockSpec((None, tf, D), lambda i, f, be, nv, nu, tok: (be[i], fidx(i, f, nu), 0))],
            out_specs=pl.BlockSpec((tm, D), lambda i, f, be, nv, nu, tok: (i, 0)),
            scratch_shapes=[pltpu.VMEM((tm, D), F32),
                            pltpu.VMEM((tm, D), BF16),
                            pltpu.SemaphoreType.DMA((1,))]),
        out_shape=jax.ShapeDtypeStruct((n_rows, D), F32),
        compiler_params=_cparams(2),
        name="moe_experts",
    )(block_e, n_valid, n_used, row_tok, x2, g, w_gate, w_up, w_down)


def _combine_body(dest_ref, yg_hbm, x_ref, tw_ref, g_ref, o_ref, ybuf, sem, *, tt, n_steps, final_norm):
    i = pl.program_id(0)

    def gather(blk, slot):
        def issue(r8, c):
            for u in range(DMA_UNROLL):
                r = r8 * DMA_UNROLL + u
                for k in range(TOP_K):
                    pltpu.make_async_copy(yg_hbm.at[pl.ds(dest_ref[(blk * tt + r) * TOP_K + k], 1), :],
                                          ybuf.at[slot, k, pl.ds(r, 1), :], sem.at[slot]).start()
            return c
        lax.fori_loop(0, tt // DMA_UNROLL, issue, 0)

    @pl.when(i == 0)
    def _():
        gather(0, 0)

    slot = i % 2

    @pl.when(i + 1 < n_steps)
    def _():
        gather(i + 1, 1 - slot)

    for r in range(tt):
        for k in range(TOP_K):
            pltpu.make_async_copy(yg_hbm.at[pl.ds(0, 1), :], ybuf.at[slot, k, pl.ds(r, 1), :],
                                  sem.at[slot]).wait()

    tw = tw_ref[...]
    y = x_ref[...] + ybuf[slot, 0] * tw[:, 0:1] + ybuf[slot, 1] * tw[:, 1:2]
    if final_norm:
        y = _rms(y, g_ref[...])
    o_ref[...] = y


def moe_combine(yg, x2, tw_col, dest, g_final, tt, final_norm):
    T, D = x2.shape
    n_steps = T // tt
    body = functools.partial(_combine_body, tt=tt, n_steps=n_steps, final_norm=final_norm)
    return pl.pallas_call(
        body,
        grid_spec=pltpu.PrefetchScalarGridSpec(
            num_scalar_prefetch=1,
            grid=(n_steps,),
            in_specs=[pl.BlockSpec(memory_space=pl.ANY),
                      pl.BlockSpec((tt, D), lambda i, d: (i, 0)),
                      pl.BlockSpec((tt, LANES), lambda i, d: (i, 0)),
                      pl.BlockSpec((1, D), lambda i, d: (0, 0))],
            out_specs=pl.BlockSpec((tt, D), lambda i, d: (i, 0)),
            scratch_shapes=[pltpu.VMEM((2, TOP_K, tt, D), F32),
                            pltpu.SemaphoreType.DMA((2,))]),
        out_shape=jax.ShapeDtypeStruct((T, D), F32),
        compiler_params=_cparams(1),
        name="moe_combine",
    )(dest, yg, x2, tw_col, g_final)


def moe_plan(top_e, tm, part):
    T = top_e.shape[1]
    TK = T * TOP_K
    experts = jnp.arange(N_EXPERTS, dtype=jnp.int32)
    rows = TK // LANES
    onehot = (top_e.T.reshape(1, rows, LANES) == experts[:, None, None]).astype(F32)
    tri = (jnp.arange(LANES)[:, None] <= jnp.arange(LANES)[None, :]).astype(F32)
    within = jnp.einsum('erk,kl->erl', onehot, tri, precision=lax.Precision.HIGHEST)
    before = (jnp.arange(rows)[:, None] < jnp.arange(rows)[None, :]).astype(F32)
    row_off = jnp.einsum('er,rs->es', within[:, :, -1], before, precision=lax.Precision.HIGHEST)
    cum = within + row_off[:, :, None]
    counts = cum[:, -1, -1].astype(jnp.int32)
    padded = (counts + tm - 1) // tm * tm
    pad_end = jnp.cumsum(padded)
    pad_start = pad_end - padded
    dest = jnp.sum(onehot * (cum - 1.0 + pad_start.astype(F32)[:, None, None]), axis=0)
    dest = dest.reshape(TK).astype(jnp.int32)
    n_blocks = -(-(TK + N_EXPERTS * (tm - 1)) // tm)
    n_rows = n_blocks * tm
    row_tok = jnp.zeros((n_rows,), jnp.int32).at[dest].set(jnp.arange(TK, dtype=jnp.int32) // TOP_K)
    blk = jnp.arange(n_blocks, dtype=jnp.int32)
    blk_onehot = ((blk[:, None] * tm >= pad_start[None, :]) & (blk[:, None] * tm < pad_end[None, :])).astype(jnp.int32)
    n_used = (pad_end[-1:] // tm).astype(jnp.int32)
    last_e = jnp.max(jnp.where(padded > 0, experts, 0))
    block_e = jnp.where(blk < n_used[0], jnp.sum(blk_onehot * experts[None, :], axis=1), last_e)
    n_valid = jnp.clip(jnp.sum(blk_onehot * (counts + pad_start)[None, :], axis=1) - blk * tm, 0, tm)
    n_valid = (n_valid + part - 1) // part
    return dest.astype(jnp.int32), row_tok, block_e.astype(jnp.int32), n_valid.astype(jnp.int32), n_used


def moe_block(x2, g, router_w, w_gate, w_up, w_down, g_final, final_norm, tm_r, tm_e, part, tf, tt):
    top_e, top_w = moe_router(x2, g, router_w.T.astype(BF16), tm_r)
    dest, row_tok, block_e, n_valid, n_used = moe_plan(top_e, tm_e, part)
    yg = moe_experts(x2, g, w_gate, w_up, w_down, block_e, n_valid, n_used, row_tok, tm_e, part, tf)
    tw_col = jnp.pad(top_w.T, ((0, 0), (0, LANES - TOP_K)))
    return moe_combine(yg, x2, tw_col, dest, g_final, tt, final_norm)


def _norm_body(x_ref, g_ref, o_ref):
    o_ref[...] = _rms(x_ref[...], g_ref[...])


def final_norm(x2, g, tm):
    T, D = x2.shape
    return pl.pallas_call(
        _norm_body,
        grid=(T // tm,),
        in_specs=[pl.BlockSpec((tm, D), lambda i: (i, 0)), pl.BlockSpec((1, D), lambda i: (0, 0))],
        out_specs=pl.BlockSpec((tm, D), lambda i: (i, 0)),
        out_shape=jax.ShapeDtypeStruct((T, D), F32),
        compiler_params=_cparams(1),
        name="final_norm",
    )(x2, g)


def _split_w_in(w):
    lo, hi = COL_SQ, COL_SQ + BA_COLS
    w_main = jnp.concatenate([w[:, :, :lo], w[:, :, hi:]], axis=2).astype(BF16)
    w_ba = jnp.pad(w[:, :, lo:hi], ((0, 0), (0, 0), (0, LANES - BA_COLS))).astype(BF16)
    return w_main, w_ba


def _pad_lanes(vec, offset):
    return jnp.pad(vec.astype(F32), (offset, LANES - offset - vec.shape[0])).reshape(1, LANES)


def mixer_layer(x2, batch, seq, layer, ln_g, w_main, w_ba, conv_w, a_log, dt_bias, gdn_norm_g, bias_tbl,
                sink_tbl, w_proj_gdn, w_proj_swa, w_out):
    proj, ba = in_proj(x2, ln_g.reshape(1, -1), w_main, w_ba, layer, tm=1024, tn=1280)
    qn, kn, vv, gb = gdn_prep(proj, ba, conv_w, _pad_lanes(a_log, GDN_V_HEADS),
                              _pad_lanes(dt_bias, GDN_V_HEADS), seq, ts=256)
    n_chunks = seq // GDN_CHUNK
    gcr = gb[:, GDN_V_HEADS:2 * GDN_V_HEADS].reshape(batch, n_chunks, GDN_CHUNK, GDN_V_HEADS)
    gcr = gcr.transpose(0, 3, 1, 2)
    o_a = gdn_core(qn, kn, vv, proj, gb, gcr, gdn_norm_g.reshape(1, -1).astype(F32), batch, seq, sb=512)
    o_b = swa_core(proj, bias_tbl, sink_tbl, batch, seq)
    y = merge_proj(o_a, o_b, w_proj_gdn, w_proj_swa, proj, layer, tm=1024, tn=512)
    return out_proj(y, w_out, x2, layer, tm=1024, tn=512)


def kernel(x, ln_mix_g, w_in, conv_w, a_log, dt_bias, gdn_norm_g, sinks, rel_bias, w_proj_gdn, w_proj_swa,
           w_out, ln_ffn_g, ffn_w_gate, ffn_w_up, ffn_w_down, router_w, moe_w_gate, moe_w_up, moe_w_down,
           ln_final_g):
    B, S, D = x.shape
    depth = w_in.shape[0]
    x2 = x.reshape(B * S, D)
    g_final = ln_final_g.reshape(1, D)
    w_main, w_ba = _split_w_in(w_in)
    w_proj_gdn, w_proj_swa, w_out = w_proj_gdn.astype(BF16), w_proj_swa.astype(BF16), w_out.astype(BF16)
    normed = False
    for i in range(depth):
        bias_tbl, sink_tbl = swa_tables(rel_bias, sinks[i])
        x2 = mixer_layer(x2, B, S, i, ln_mix_g[i], w_main, w_ba, conv_w[i], a_log[i], dt_bias[i],
                         gdn_norm_g[i], bias_tbl, sink_tbl, w_proj_gdn, w_proj_swa, w_out)
        g_ffn = ln_ffn_g[i].reshape(1, D)
        j = i // 2
        if i % 2 == 0:
            f = ffn_w_gate.shape[2]
            fpad = (-f) % 512
            wg = jnp.pad(ffn_w_gate[j], ((0, 0), (0, fpad))).astype(BF16)
            wu = jnp.pad(ffn_w_up[j], ((0, 0), (0, fpad))).astype(BF16)
            wd = jnp.pad(ffn_w_down[j], ((0, fpad), (0, 0))).astype(BF16)
            x2 = dense_ffn(x2, g_ffn, wg, wu, wd, tm=1024, tf=512)
        else:
            last = i == depth - 1
            x2 = moe_block(x2, g_ffn, router_w[j], moe_w_gate[j], moe_w_up[j], moe_w_down[j],
                           g_final, last, tm_r=512, tm_e=768, part=256, tf=512, tt=256)
            normed = last
    if not normed:
        x2 = final_norm(x2, g_final, tm=512)
    return x2.reshape(B, S, D)
```

```python
import functools
import math

import jax
import jax.numpy as jnp
from jax import lax
from jax.experimental import pallas as pl
from jax.experimental.pallas import tpu as pltpu

F32 = jnp.float32
BF16 = jnp.bfloat16

D_MODEL = 2048
GDN_QK_HEADS = 8
GDN_V_HEADS = 16
GDN_HEAD_DIM = 128
GDN_CONV = 4
GDN_CHUNK = 64
SWA_Q_HEADS = 32
SWA_KV_HEADS = 4
SWA_HEAD_DIM = 64
SWA_WINDOW = 128
REL_BUCKETS = 32
REL_MAX_DIST = 128
N_EXPERTS = 8
TOP_K = 2
NORM_EPS = 1e-6
GDN_KEY_DIM = GDN_QK_HEADS * GDN_HEAD_DIM
GDN_VAL_DIM = GDN_V_HEADS * GDN_HEAD_DIM
SWA_Q_DIM = SWA_Q_HEADS * SWA_HEAD_DIM
SWA_KV_DIM = SWA_KV_HEADS * SWA_HEAD_DIM

LANES = 128
VMEM_LIMIT = 56 * 1024 * 1024

COL_Q = 0
COL_K = COL_Q + GDN_KEY_DIM
COL_V = COL_K + GDN_KEY_DIM
COL_Z = COL_V + GDN_VAL_DIM
COL_SQ = COL_Z + GDN_VAL_DIM
COL_SK = COL_SQ + SWA_Q_DIM
COL_SV = COL_SK + SWA_KV_DIM
COL_GA = COL_SV + SWA_KV_DIM
COL_GB = COL_GA + D_MODEL
PROJ_COLS = COL_GB + D_MODEL
BA_COLS = 2 * GDN_V_HEADS
GDN_CONV_DIM = 2 * GDN_KEY_DIM + GDN_VAL_DIM

MASKED = -1e30
DMA_UNROLL = 8


def _cparams(n_axes):
    return pltpu.CompilerParams(dimension_semantics=("arbitrary",) * n_axes,
                                vmem_limit_bytes=VMEM_LIMIT)


def _rms(x, g):
    return x * lax.rsqrt(jnp.mean(x * x, axis=-1, keepdims=True) + NORM_EPS) * g


def _silu(x):
    return x * jax.nn.sigmoid(x)


def _dot(a, b):
    return jnp.dot(a, b, preferred_element_type=F32)


def _dot_nt(a, b):
    return lax.dot_general(a, b, (((1,), (1,)), ((), ())), preferred_element_type=F32)


def _dot_tn(a, b):
    return lax.dot_general(a, b, (((0,), (0,)), ((), ())), preferred_element_type=F32)


def _inproj_body(x_ref, g_ref, w_ref, wba_ref, o_ref, oba_ref, h_ref):
    @pl.when(pl.program_id(1) == 0)
    def _():
        h = _rms(x_ref[...], g_ref[...]).astype(BF16)
        h_ref[...] = h
        oba_ref[...] = _dot(h, wba_ref[...])

    o_ref[...] = _dot(h_ref[...], w_ref[...]).astype(o_ref.dtype)


def in_proj(x2, g, w_main, w_ba, layer, tm, tn):
    T, D = x2.shape
    N = w_main.shape[2]
    return pl.pallas_call(
        _inproj_body,
        grid=(T // tm, N // tn),
        in_specs=[pl.BlockSpec((tm, D), lambda i, j: (i, 0)),
                  pl.BlockSpec((1, D), lambda i, j: (0, 0)),
                  pl.BlockSpec((None, D, tn), lambda i, j: (layer, 0, j)),
                  pl.BlockSpec((None, D, LANES), lambda i, j: (layer, 0, 0))],
        out_specs=[pl.BlockSpec((tm, tn), lambda i, j: (i, j)),
                   pl.BlockSpec((tm, LANES), lambda i, j: (i, 0))],
        out_shape=[jax.ShapeDtypeStruct((T, N), BF16), jax.ShapeDtypeStruct((T, LANES), F32)],
        scratch_shapes=[pltpu.VMEM((tm, D), BF16)],
        compiler_params=_cparams(2),
        name="in_proj",
    )(x2, g, w_main, w_ba)


def _prep_body(x_ref, halo_ref, cw_ref, ba_ref, alog_ref, dtb_ref, q_ref, k_ref, v_ref, gb_ref,
               *, ts, seq, n_qk, n_v, chunk):
    first = (pl.program_id(0) * ts) % seq == 0
    for c in range(2 * n_qk + n_v):
        cs = slice(c * LANES, (c + 1) * LANES)
        cur = x_ref[:, cs].astype(F32)
        hal = halo_ref[8:16, cs].astype(F32)
        hal = jnp.where(first, 0.0, hal)
        ext = jnp.concatenate([hal, cur], axis=0)
        y = ext[5:5 + ts] * cw_ref[0:1, cs]
        y = y + ext[6:6 + ts] * cw_ref[1:2, cs]
        y = y + ext[7:7 + ts] * cw_ref[2:3, cs]
        y = y + cur * cw_ref[3:4, cs]
        y = _silu(y)
        if c < 2 * n_qk:
            y = y * lax.rsqrt(jnp.sum(y * y, axis=-1, keepdims=True) + 1e-6)
        if c < n_qk:
            q_ref[:, cs] = (y * (GDN_HEAD_DIM ** -0.5)).astype(q_ref.dtype)
        elif c < 2 * n_qk:
            k_ref[:, (c - n_qk) * LANES:(c - n_qk + 1) * LANES] = y.astype(k_ref.dtype)
        else:
            v_ref[:, (c - 2 * n_qk) * LANES:(c - 2 * n_qk + 1) * LANES] = y.astype(v_ref.dtype)

    ba = ba_ref[...]
    lane = lax.broadcasted_iota(jnp.int32, ba.shape, 1)
    pos = lax.broadcasted_iota(jnp.int32, ba.shape, 0) % chunk
    beta = jax.nn.sigmoid(ba)
    t = ba + dtb_ref[...]
    softplus = jnp.maximum(t, 0.0) + jnp.log1p(jnp.exp(-jnp.abs(t)))
    gc = -jnp.exp(alog_ref[...]) * softplus
    s = 1
    while s < chunk:
        gc = gc + jnp.where(pos >= s, pltpu.roll(gc, s, 0), 0.0)
        s *= 2
    gb_ref[...] = jnp.where(lane < n_v, beta, jnp.where(lane < 2 * n_v, gc, 0.0))


def gdn_prep(proj, ba, conv_w, alog_pad, dtb_pad, seq, ts):
    T = proj.shape[0]
    n_qk, n_v = GDN_QK_HEADS, GDN_V_HEADS
    body = functools.partial(_prep_body, ts=ts, seq=seq, n_qk=n_qk, n_v=n_v, chunk=GDN_CHUNK)
    return pl.pallas_call(
        body,
        grid=(T // ts,),
        in_specs=[pl.BlockSpec((ts, GDN_CONV_DIM), lambda i: (i, 0)),
                  pl.BlockSpec((16, GDN_CONV_DIM), lambda i: (jnp.maximum(i * (ts // 16) - 1, 0), 0)),
                  pl.BlockSpec((GDN_CONV, GDN_CONV_DIM), lambda i: (0, 0)),
                  pl.BlockSpec((ts, LANES), lambda i: (i, 0)),
                  pl.BlockSpec((1, LANES), lambda i: (0, 0)),
                  pl.BlockSpec((1, LANES), lambda i: (0, 0))],
        out_specs=[pl.BlockSpec((ts, GDN_KEY_DIM), lambda i: (i, 0)),
                   pl.BlockSpec((ts, GDN_KEY_DIM), lambda i: (i, 0)),
                   pl.BlockSpec((ts, GDN_VAL_DIM), lambda i: (i, 0)),
                   pl.BlockSpec((ts, LANES), lambda i: (i, 0))],
        out_shape=[jax.ShapeDtypeStruct((T, GDN_KEY_DIM), BF16),
                   jax.ShapeDtypeStruct((T, GDN_KEY_DIM), BF16),
                   jax.ShapeDtypeStruct((T, GDN_VAL_DIM), BF16),
                   jax.ShapeDtypeStruct((T, LANES), F32)],
        compiler_params=_cparams(1),
        name="gdn_prep",
    )(proj, proj, conv_w, ba, alog_pad, dtb_pad)


def _gdn_body(q_ref, k_ref, v_ref, z_ref, gb_ref, gcr_ref, gn_ref, o_ref,
              u_s, wq_s, qkd_s, gl_s, st_s, *, n_chunks, n_qk, n_v):
    C = GDN_CHUNK
    hd = GDN_HEAD_DIM
    assert n_v == 2 * n_qk and 2 * C == LANES and hd == LANES

    @pl.when(pl.program_id(1) == 0)
    def _():
        st_s[...] = jnp.zeros_like(st_s)

    low = lax.broadcasted_iota(jnp.int32, (1, LANES), 1) < C
    row = lax.broadcasted_iota(jnp.int32, (C, LANES), 0)
    col = jnp.bitwise_and(lax.broadcasted_iota(jnp.int32, (C, LANES), 1), C - 1)
    causal = row >= col
    strict = row > col
    zero_blk = jnp.zeros((C, hd), BF16)

    hs = range(n_qk)
    hvs = range(n_v)

    def block_diag(x):
        z = jnp.zeros_like(x)
        return jnp.concatenate([jnp.where(low, x, z), jnp.where(low, z, x)], axis=0)

    def phase1(c, carry):
        rows = pl.ds(pl.multiple_of(c * C, C), C)
        rows2 = pl.ds(pl.multiple_of(c * 2 * C, 2 * C), C)
        rows2b = pl.ds(pl.multiple_of(c * 2 * C, 2 * C) + C, C)
        gbc = gb_ref[rows, :]
        q = [q_ref[rows, h * hd:(h + 1) * hd] for h in hs]
        k = [k_ref[rows, h * hd:(h + 1) * hd] for h in hs]
        kdup = [jnp.concatenate([k[h], k[h]], axis=0) for h in hs]
        kk = [_dot_nt(k[h], kdup[h]) for h in hs]
        qk = [_dot_nt(q[h], kdup[h]) for h in hs]
        bcol1 = [gbc[:, hv:hv + 1] for hv in hvs]
        gcol1 = [gbc[:, n_v + hv:n_v + hv + 1] for hv in hvs]
        glast = [gbc[C - 1:C, n_v + hv:n_v + hv + 1] for hv in hvs]
        bcol = [jnp.where(low, bcol1[2 * h], bcol1[2 * h + 1]) for h in hs]
        gcol = [jnp.where(low, gcol1[2 * h], gcol1[2 * h + 1]) for h in hs]
        decay = [jnp.exp(jnp.where(causal, gcol[h] - gcr_ref[0, h, pl.ds(c, 1), :], -jnp.inf)) for h in hs]
        x = [jnp.where(strict, kk[h] * bcol[h] * decay[h], 0.0) * -1.0 for h in hs]
        nn = list(x)
        xb = [v.astype(BF16) for v in x]
        x = [_dot(xb[h], block_diag(xb[h])) for h in hs]
        egc = [jnp.exp(g) for g in gcol1]
        for h in hs:
            qf = q[h].astype(F32)
            kf = k[h].astype(F32)
            kd = []
            for hv in (2 * h, 2 * h + 1):
                wq_s[hv, rows2b, :] = (qf * egc[hv]).astype(BF16)
                kd.append(kf * jnp.exp(glast[hv] - gcol1[hv]))
                gl_s[hv, pl.ds(c, 1), :] = jnp.broadcast_to(jnp.exp(glast[hv]), (1, LANES))
            qkd_s[h, c, 0:C, :] = (qk[h] * decay[h]).astype(BF16)
            qkd_s[h, c, C:3 * C, :] = jnp.concatenate(kd, axis=0).T.astype(BF16)
        p = 2
        while 2 * p < C:
            xb = [v.astype(BF16) for v in x]
            res = [_dot(jnp.concatenate([xb[h], nn[h].astype(BF16)], axis=0), block_diag(xb[h])) for h in hs]
            nn = [nn[h] + x[h] + res[h][C:2 * C] for h in hs]
            x = [res[h][0:C] for h in hs]
            p *= 2
        nn = [nn[h] + x[h] + _dot(nn[h].astype(BF16), block_diag(x[h].astype(BF16))) for h in hs]
        nb = [n.astype(BF16) for n in nn]
        vb = [v_ref[rows, hv * hd:(hv + 1) * hd].astype(F32) * bcol1[hv] for hv in hvs]
        kbg = [k[hv // 2].astype(F32) * (bcol1[hv] * egc[hv]) for hv in hvs]
        z = zero_blk
        uw = [_dot(nb[h], jnp.concatenate([
            jnp.concatenate([vb[2 * h].astype(BF16), z, kbg[2 * h].astype(BF16), z], axis=1),
            jnp.concatenate([z, vb[2 * h + 1].astype(BF16), z, kbg[2 * h + 1].astype(BF16)], axis=1)], axis=0))
            for h in hs]
        for hv in hvs:
            h, j = hv // 2, hv % 2
            u_s[hv, rows, :] = vb[hv] + uw[h][:, j * hd:(j + 1) * hd]
            wq_s[hv, rows2, :] = (kbg[hv] + uw[h][:, (2 + j) * hd:(3 + j) * hd]).astype(BF16)
        return carry

    lax.fori_loop(0, n_chunks, phase1, 0)

    def phase2(c, carry):
        rows = pl.ds(pl.multiple_of(c * C, C), C)
        rows_wq = pl.ds(pl.multiple_of(c * 2 * C, 2 * C), 2 * C)
        sb = [st_s[hv].astype(BF16) for hv in hvs]
        ws = [_dot(wq_s[hv, rows_wq, :], sb[hv]) for hv in hvs]
        vnb = [(u_s[hv, rows, :] - ws[hv][0:C]).astype(BF16) for hv in hvs]
        z = zero_blk
        ov = [_dot(qkd_s[h, c], jnp.concatenate([jnp.concatenate([vnb[2 * h], z], axis=1),
                                                 jnp.concatenate([z, vnb[2 * h + 1]], axis=1)], axis=0))
              for h in hs]
        for hv in hvs:
            h, j = hv // 2, hv % 2
            st_s[hv] = st_s[hv] * gl_s[hv, pl.ds(c, 1), :] + ov[h][C:3 * C, j * hd:(j + 1) * hd]
        for hv in hvs:
            h, j = hv // 2, hv % 2
            o = ws[hv][C:2 * C] + ov[h][0:C, j * hd:(j + 1) * hd]
            o = o * lax.rsqrt(jnp.mean(o * o, axis=-1, keepdims=True) + NORM_EPS) * gn_ref[...]
            o = o * _silu(z_ref[rows, hv * hd:(hv + 1) * hd].astype(F32))
            o_ref[rows, hv * hd:(hv + 1) * hd] = o.astype(o_ref.dtype)
        return carry

    lax.fori_loop(0, n_chunks, phase2, 0)


def gdn_core(qn, kn, vv, proj, gb, gcr, gnorm, batch, seq, sb):
    T = qn.shape[0]
    n_qk, n_v, hd, C = GDN_QK_HEADS, GDN_V_HEADS, GDN_HEAD_DIM, GDN_CHUNK
    nc = sb // C
    nsb = seq // sb
    zc = COL_Z // (n_v * hd)
    body = functools.partial(_gdn_body, n_chunks=nc, n_qk=n_qk, n_v=n_v)
    blk = lambda b, s: (b * nsb + s, 0)
    return pl.pallas_call(
        body,
        grid=(batch, nsb),
        in_specs=[pl.BlockSpec((sb, n_qk * hd), blk),
                  pl.BlockSpec((sb, n_qk * hd), blk),
                  pl.BlockSpec((sb, n_v * hd), blk),
                  pl.BlockSpec((sb, n_v * hd), lambda b, s: (b * nsb + s, zc)),
                  pl.BlockSpec((sb, LANES), blk),
                  pl.BlockSpec((1, n_qk, nc, 2 * C), lambda b, s: (b, 0, s, 0)),
                  pl.BlockSpec((1, hd), lambda b, s: (0, 0))],
        out_specs=pl.BlockSpec((sb, n_v * hd), blk),
        out_shape=jax.ShapeDtypeStruct((T, n_v * hd), BF16),
        scratch_shapes=[pltpu.VMEM((n_v, sb, hd), F32),
                        pltpu.VMEM((n_v, 2 * sb, hd), BF16),
                        pltpu.VMEM((n_qk, nc, 3 * C, 2 * C), BF16),
                        pltpu.VMEM((n_v, nc, LANES), F32),
                        pltpu.VMEM((n_v, hd, hd), F32)],
        compiler_params=_cparams(2),
        name="gdn_core",
    )(qn, kn, vv, proj, gb, gcr, gnorm)


def _swa_body(q_ref, kp_ref, kc_ref, vp_ref, vc_ref, bias_ref, sink_ref, o_ref):
    W, dh = SWA_WINDOW, SWA_HEAD_DIM
    pairs = (SWA_Q_HEADS // SWA_KV_HEADS) // 2
    kvs = range(SWA_KV_HEADS)
    zero = jnp.zeros((2 * W, dh), BF16)
    one = jnp.ones((2 * W, dh), BF16)
    low = lax.broadcasted_iota(jnp.int32, (1, LANES), 1) < dh

    logits = []
    for hk in kvs:
        ks = slice(hk * dh, (hk + 1) * dh)
        kb = jnp.concatenate([kp_ref[:, ks], kc_ref[:, ks]], axis=0)
        krhs = jnp.concatenate([jnp.concatenate([kb, zero], axis=1),
                                jnp.concatenate([zero, kb], axis=1)], axis=0)
        qs = jnp.concatenate([q_ref[:, (hk * pairs + p) * LANES:(hk * pairs + p + 1) * LANES]
                              for p in range(pairs)], axis=0)
        qs = qs * jnp.asarray(dh ** -0.5, BF16)
        logits.append(_dot_nt(qs, krhs) + bias_ref[0, hk])

    outs, sinks = [], []
    for hk in kvs:
        ks = slice(hk * dh, (hk + 1) * dh)
        vb = jnp.concatenate([vp_ref[:, ks], vc_ref[:, ks]], axis=0)
        vrhs = jnp.concatenate([jnp.concatenate([vb, zero, one, zero], axis=1),
                                jnp.concatenate([zero, vb, zero, one], axis=1)], axis=0)
        ps, ms = [], []
        for c in range(2):
            lg = logits[hk][:, c * 2 * W:(c + 1) * 2 * W]
            sk = sink_ref[hk, c]
            m = jnp.max(jnp.maximum(jnp.maximum(lg[:, :W], lg[:, W:]), sk), axis=-1, keepdims=True)
            ps.append(jnp.exp(lg - m).astype(BF16))
            ms.append(sk - m)
        outs.append(_dot(jnp.concatenate(ps, axis=1), vrhs))
        sinks.append(jnp.exp(jnp.where(low, ms[0], ms[1])))

    for hk in kvs:
        out = outs[hk][:, :LANES] / (outs[hk][:, LANES:] + sinks[hk])
        for p in range(pairs):
            o_ref[:, (hk * pairs + p) * LANES:(hk * pairs + p + 1) * LANES] = (
                out[p * W:(p + 1) * W].astype(o_ref.dtype))


def swa_core(proj, bias_tbl, sink_tbl, batch, seq):
    T = proj.shape[0]
    W = SWA_WINDOW
    nb = seq // W
    qc, kc, vc = COL_SQ // SWA_Q_DIM, COL_SK // SWA_KV_DIM, COL_SV // SWA_KV_DIM
    cur = lambda b, n: b * nb + n
    prev = lambda b, n: jnp.maximum(b * nb + n - 1, 0)
    return pl.pallas_call(
        _swa_body,
        grid=(batch, nb),
        in_specs=[pl.BlockSpec((W, SWA_Q_DIM), lambda b, n: (cur(b, n), qc)),
                  pl.BlockSpec((W, SWA_KV_DIM), lambda b, n: (prev(b, n), kc)),
                  pl.BlockSpec((W, SWA_KV_DIM), lambda b, n: (cur(b, n), kc)),
                  pl.BlockSpec((W, SWA_KV_DIM), lambda b, n: (prev(b, n), vc)),
                  pl.BlockSpec((W, SWA_KV_DIM), lambda b, n: (cur(b, n), vc)),
                  pl.BlockSpec((1,) + bias_tbl.shape[1:], lambda b, n: (jnp.minimum(n, 1), 0, 0, 0)),
                  pl.BlockSpec(sink_tbl.shape, lambda b, n: (0, 0, 0, 0))],
        out_specs=pl.BlockSpec((W, SWA_Q_DIM), lambda b, n: (cur(b, n), 0)),
        out_shape=jax.ShapeDtypeStruct((T, SWA_Q_DIM), BF16),
        compiler_params=_cparams(2),
        name="swa_core",
    )(proj, proj, proj, proj, proj, bias_tbl, sink_tbl)


def _t5_bucket(dist):
    max_exact = REL_BUCKETS // 2
    large = max_exact + (jnp.log(jnp.maximum(dist, 1).astype(F32) / max_exact)
                         / math.log(REL_MAX_DIST / max_exact) * (REL_BUCKETS - max_exact)).astype(jnp.int32)
    large = jnp.minimum(large, REL_BUCKETS - 1)
    return jnp.where(dist < max_exact, dist, large)


def swa_tables(rel_bias, sinks):
    W, hq, hkv = SWA_WINDOW, SWA_Q_HEADS, SWA_KV_HEADS
    pairs = (hq // hkv) // 2
    by_dist = rel_bias.astype(F32)[_t5_bucket(jnp.arange(W))]
    period = 3 * W
    vec = jnp.concatenate([jnp.full((1, hq), MASKED, F32), by_dist[::-1],
                           jnp.full((period - W - 1, hq), MASKED, F32)], axis=0).T
    tbl = jnp.tile(vec, (1, W))[:, :W * (period - 1)].reshape(hq, W, period - 1)[:, :, :2 * W]
    first = jnp.where(jnp.arange(2 * W)[None, None, :] < W, MASKED, tbl)
    tbl = jnp.stack([first, tbl])
    tbl = tbl.reshape(2, hkv, pairs, 2, W, 2 * W).transpose(0, 1, 2, 4, 3, 5)
    tbl = tbl.reshape(2, hkv, pairs * W, 4 * W)
    sk = sinks.astype(F32).reshape(hkv, pairs, 2).transpose(0, 2, 1)
    sk = jnp.broadcast_to(sk[:, :, :, None, None], (hkv, 2, pairs, W, LANES)).reshape(hkv, 2, pairs * W, LANES)
    return tbl, sk


def _merge_body(oa_ref, ob_ref, wa_ref, wb_ref, ga_ref, gb_ref, y_ref):
    a = _dot(oa_ref[...], wa_ref[...])
    b = _dot(ob_ref[...], wb_ref[...])
    y = jax.nn.sigmoid(ga_ref[...].astype(F32)) * a + jax.nn.sigmoid(gb_ref[...].astype(F32)) * b
    y_ref[...] = y.astype(y_ref.dtype)


def merge_proj(o_a, o_b, w_a, w_b, proj, layer, tm, tn):
    T, K = o_a.shape
    N = w_a.shape[2]
    ga0, gb0 = COL_GA // tn, COL_GB // tn
    return pl.pallas_call(
        _merge_body,
        grid=(T // tm, N // tn),
        in_specs=[pl.BlockSpec((tm, K), lambda i, j: (i, 0)),
                  pl.BlockSpec((tm, K), lambda i, j: (i, 0)),
                  pl.BlockSpec((None, K, tn), lambda i, j: (layer, 0, j)),
                  pl.BlockSpec((None, K, tn), lambda i, j: (layer, 0, j)),
                  pl.BlockSpec((tm, tn), lambda i, j: (i, ga0 + j)),
                  pl.BlockSpec((tm, tn), lambda i, j: (i, gb0 + j))],
        out_specs=pl.BlockSpec((tm, tn), lambda i, j: (i, j)),
        out_shape=jax.ShapeDtypeStruct((T, N), BF16),
        compiler_params=_cparams(2),
        name="merge_proj",
    )(o_a, o_b, w_a, w_b, proj, proj)


def _resid_body(y_ref, w_ref, x_ref, o_ref):
    o_ref[...] = x_ref[...] + _dot(y_ref[...], w_ref[...])


def out_proj(y, w, x2, layer, tm, tn):
    T, K = y.shape
    N = w.shape[2]
    return pl.pallas_call(
        _resid_body,
        grid=(T // tm, N // tn),
        in_specs=[pl.BlockSpec((tm, K), lambda i, j: (i, 0)),
                  pl.BlockSpec((None, K, tn), lambda i, j: (layer, 0, j)),
                  pl.BlockSpec((tm, tn), lambda i, j: (i, j))],
        out_specs=pl.BlockSpec((tm, tn), lambda i, j: (i, j)),
        out_shape=jax.ShapeDtypeStruct((T, N), F32),
        compiler_params=_cparams(2),
        name="out_proj",
    )(y, w, x2)


def _ffn_body(x_ref, g_ref, wg_ref, wu_ref, wd_ref, o_ref, h_ref):
    @pl.when(pl.program_id(1) == 0)
    def _():
        x = x_ref[...]
        h_ref[...] = _rms(x, g_ref[...]).astype(BF16)
        o_ref[...] = x

    h = h_ref[...]
    mid = (_silu(_dot(h, wg_ref[...])) * _dot(h, wu_ref[...])).astype(BF16)
    o_ref[...] += _dot(mid, wd_ref[...])


def dense_ffn(x2, g, w_gate, w_up, w_down, tm, tf):
    T, D = x2.shape
    F = w_gate.shape[1]
    return pl.pallas_call(
        _ffn_body,
        grid=(T // tm, F // tf),
        in_specs=[pl.BlockSpec((tm, D), lambda i, f: (i, 0)),
                  pl.BlockSpec((1, D), lambda i, f: (0, 0)),
                  pl.BlockSpec((D, tf), lambda i, f: (0, f)),
                  pl.BlockSpec((D, tf), lambda i, f: (0, f)),
                  pl.BlockSpec((tf, D), lambda i, f: (f, 0))],
        out_specs=pl.BlockSpec((tm, D), lambda i, f: (i, 0)),
        out_shape=jax.ShapeDtypeStruct((T, D), F32),
        scratch_shapes=[pltpu.VMEM((tm, D), BF16)],
        compiler_params=_cparams(2),
        name="dense_ffn",
    )(x2, g, w_gate, w_up, w_down)


def _router_body(x_ref, g_ref, rwt_ref, e_ref, w_ref):
    h = _rms(x_ref[...], g_ref[...]).astype(BF16)
    lt = _dot_nt(rwt_ref[...], h)
    idx = lax.broadcasted_iota(jnp.int32, lt.shape, 0)
    m1 = jnp.max(lt, axis=0, keepdims=True)
    i1 = jnp.min(jnp.where(lt == m1, idx, N_EXPERTS), axis=0, keepdims=True)
    lt2 = jnp.where(idx == i1, -jnp.inf, lt)
    m2 = jnp.max(lt2, axis=0, keepdims=True)
    i2 = jnp.min(jnp.where(lt2 == m2, idx, N_EXPERTS), axis=0, keepdims=True)
    e2 = jnp.exp(m2 - m1)
    den = 1.0 + e2
    e_ref[...] = jnp.concatenate([i1, i2], axis=0)
    w_ref[...] = jnp.concatenate([1.0 / den, e2 / den], axis=0)


def moe_router(x2, g, rw_t, tm):
    T, D = x2.shape
    return pl.pallas_call(
        _router_body,
        grid=(T // tm,),
        in_specs=[pl.BlockSpec((tm, D), lambda i: (i, 0)),
                  pl.BlockSpec((1, D), lambda i: (0, 0)),
                  pl.BlockSpec((N_EXPERTS, D), lambda i: (0, 0))],
        out_specs=[pl.BlockSpec((TOP_K, tm), lambda i: (0, i)),
                   pl.BlockSpec((TOP_K, tm), lambda i: (0, i))],
        out_shape=[jax.ShapeDtypeStruct((TOP_K, T), jnp.int32),
                   jax.ShapeDtypeStruct((TOP_K, T), F32)],
        compiler_params=_cparams(1),
        name="moe_router",
    )(x2, g, rw_t)


def _moe_body(be_ref, nv_ref, nu_ref, tok_ref, x_hbm, g_ref, wg_ref, wu_ref, wd_ref, o_ref,
              xbuf, h_ref, sem, *, tm, part):
    i = pl.program_id(0)
    f = pl.program_id(1)
    n_used = nu_ref[0]

    def row_copy(idx, r):
        return pltpu.make_async_copy(x_hbm.at[pl.ds(idx, 1), :], xbuf.at[pl.ds(r, 1), :], sem.at[0])

    def gather(blk):
        def issue(r8, c):
            for u in range(DMA_UNROLL):
                r = r8 * DMA_UNROLL + u
                row_copy(tok_ref[blk * tm + r], r).start()
            return c
        lax.fori_loop(0, nv_ref[blk] * (part // DMA_UNROLL), issue, 0)

    def load_rows(rows):
        for r in range(rows):
            row_copy(0, r).wait()
        h_ref[0:rows, :] = _rms(xbuf[0:rows, :], g_ref[...]).astype(BF16)

    def compute(rows):
        h = h_ref[0:rows, :]
        gate = _dot(h, wg_ref[...].astype(BF16))
        up = _dot(h, wu_ref[...].astype(BF16))
        mid = (_silu(gate) * up).astype(BF16)
        o_ref[0:rows, :] += _dot(mid, wd_ref[...].astype(BF16))

    active = i < n_used
    pieces = nv_ref[i]

    @pl.when((f == 0) & (i == 0))
    def _():
        gather(0)

    for p in range(1, tm // part + 1):
        pl.when((f == 0) & active & (pieces == p))(functools.partial(load_rows, p * part))

    @pl.when((f == 1) & (i + 1 < n_used))
    def _():
        gather(i + 1)

    @pl.when(f == 0)
    def _():
        o_ref[...] = jnp.zeros_like(o_ref)

    for p in range(1, tm // part + 1):
        pl.when(active & (pieces == p))(functools.partial(compute, p * part))


def moe_experts(x2, g, w_gate, w_up, w_down, block_e, n_valid, n_used, row_tok, tm, part, tf):
    T, D = x2.shape
    E, _, F = w_gate.shape
    n_rows = row_tok.shape[0]
    nf = F // tf
    assert nf >= 2 and tm % part == 0 and part % DMA_UNROLL == 0

    def fidx(i, f, nu):
        return jnp.where(i < nu[0], f, nf - 1)

    body = functools.partial(_moe_body, tm=tm, part=part)
    return pl.pallas_call(
        body,
        grid_spec=pltpu.PrefetchScalarGridSpec(
            num_scalar_prefetch=4,
            grid=(n_rows // tm, nf),
            in_specs=[pl.BlockSpec(memory_space=pl.ANY),
                      pl.BlockSpec((1, D), lambda i, f, be, nv, nu, tok: (0, 0)),
                      pl.BlockSpec((None, D, tf), lambda i, f, be, nv, nu, tok: (be[i], 0, fidx(i, f, nu))),
                      pl.BlockSpec((None, D, tf), lambda i, f, be, nv, nu, tok: (be[i], 0, fidx(i, f, nu))),
                      pl.BlockSpec((None, tf, D), lambda i, f, be, nv, nu, tok: (be[i], fidx(i, f, nu), 0))],
            out_specs=pl.BlockSpec((tm, D), lambda i, f, be, nv, nu, tok: (i, 0)),
            scratch_shapes=[pltpu.VMEM((tm, D), F32),
                            pltpu.VMEM((tm, D), BF16),
                            pltpu.SemaphoreType.DMA((1,))]),
        out_shape=jax.ShapeDtypeStruct((n_rows, D), F32),
        compiler_params=_cparams(2),
        name="moe_experts",
    )(block_e, n_valid, n_used, row_tok, x2, g, w_gate, w_up, w_down)


def _combine_body(dest_ref, yg_hbm, x_ref, tw_ref, g_ref, o_ref, ybuf, sem, *, tt, n_steps, final_norm):
    i = pl.program_id(0)

    def gather(blk, slot):
        def issue(r8, c):
            for u in range(DMA_UNROLL):
                r = r8 * DMA_UNROLL + u
                for k in range(TOP_K):
                    pltpu.make_async_copy(yg_hbm.at[pl.ds(dest_ref[(blk * tt + r) * TOP_K + k], 1), :],
                                          ybuf.at[slot, k, pl.ds(r, 1), :], sem.at[slot]).start()
            return c
        lax.fori_loop(0, tt // DMA_UNROLL, issue, 0)

    @pl.when(i == 0)
    def _():
        gather(0, 0)

    slot = i % 2

    @pl.when(i + 1 < n_steps)
    def _():
        gather(i + 1, 1 - slot)

    for r in range(tt):
        for k in range(TOP_K):
            pltpu.make_async_copy(yg_hbm.at[pl.ds(0, 1), :], ybuf.at[slot, k, pl.ds(r, 1), :],
                                  sem.at[slot]).wait()

    tw = tw_ref[...]
    y = x_ref[...] + ybuf[slot, 0] * tw[:, 0:1] + ybuf[slot, 1] * tw[:, 1:2]
    if final_norm:
        y = _rms(y, g_ref[...])
    o_ref[...] = y


def moe_combine(yg, x2, tw_col, dest, g_final, tt, final_norm):
    T, D = x2.shape
    n_steps = T // tt
    body = functools.partial(_combine_body, tt=tt, n_steps=n_steps, final_norm=final_norm)
    return pl.pallas_call(
        body,
        grid_spec=pltpu.PrefetchScalarGridSpec(
            num_scalar_prefetch=1,
            grid=(n_steps,),
            in_specs=[pl.BlockSpec(memory_space=pl.ANY),
                      pl.BlockSpec((tt, D), lambda i, d: (i, 0)),
                      pl.BlockSpec((tt, LANES), lambda i, d: (i, 0)),
                      pl.BlockSpec((1, D), lambda i, d: (0, 0))],
            out_specs=pl.BlockSpec((tt, D), lambda i, d: (i, 0)),
            scratch_shapes=[pltpu.VMEM((2, TOP_K, tt, D), F32),
                            pltpu.SemaphoreType.DMA((2,))]),
        out_shape=jax.ShapeDtypeStruct((T, D), F32),
        compiler_params=_cparams(1),
        name="moe_combine",
    )(dest, yg, x2, tw_col, g_final)


def moe_plan(top_e, tm, part):
    T = top_e.shape[1]
    TK = T * TOP_K
    experts = jnp.arange(N_EXPERTS, dtype=jnp.int32)
    rows = TK // LANES
    onehot = (top_e.T.reshape(1, rows, LANES) == experts[:, None, None]).astype(F32)
    tri = (jnp.arange(LANES)[:, None] <= jnp.arange(LANES)[None, :]).astype(F32)
    within = jnp.einsum('erk,kl->erl', onehot, tri, precision=lax.Precision.HIGHEST)
    before = (jnp.arange(rows)[:, None] < jnp.arange(rows)[None, :]).astype(F32)
    row_off = jnp.einsum('er,rs->es', within[:, :, -1], before, precision=lax.Precision.HIGHEST)
    cum = within + row_off[:, :, None]
    counts = cum[:, -1, -1].astype(jnp.int32)
    padded = (counts + tm - 1) // tm * tm
    pad_end = jnp.cumsum(padded)
    pad_start = pad_end - padded
    dest = jnp.sum(onehot * (cum - 1.0 + pad_start.astype(F32)[:, None, None]), axis=0)
    dest = dest.reshape(TK).astype(jnp.int32)
    n_blocks = -(-(TK + N_EXPERTS * (tm - 1)) // tm)
    n_rows = n_blocks * tm
    row_tok = jnp.zeros((n_rows,), jnp.int32).at[dest].set(jnp.arange(TK, dtype=jnp.int32) // TOP_K)
    blk = jnp.arange(n_blocks, dtype=jnp.int32)
    blk_onehot = ((blk[:, None] * tm >= pad_start[None, :]) & (blk[:, None] * tm < pad_end[None, :])).astype(jnp.int32)
    n_used = (pad_end[-1:] // tm).astype(jnp.int32)
    last_e = jnp.max(jnp.where(padded > 0, experts, 0))
    block_e = jnp.where(blk < n_used[0], jnp.sum(blk_onehot * experts[None, :], axis=1), last_e)
    n_valid = jnp.clip(jnp.sum(blk_onehot * (counts + pad_start)[None, :], axis=1) - blk * tm, 0, tm)
    n_valid = (n_valid + part - 1) // part
    return dest.astype(jnp.int32), row_tok, block_e.astype(jnp.int32), n_valid.astype(jnp.int32), n_used


def moe_block(x2, g, router_w, w_gate, w_up, w_down, g_final, final_norm, tm_r, tm_e, part, tf, tt):
    top_e, top_w = moe_router(x2, g, router_w.T.astype(BF16), tm_r)
    dest, row_tok, block_e, n_valid, n_used = moe_plan(top_e, tm_e, part)
    yg = moe_experts(x2, g, w_gate, w_up, w_down, block_e, n_valid, n_used, row_tok, tm_e, part, tf)
    tw_col = jnp.pad(top_w.T, ((0, 0), (0, LANES - TOP_K)))
    return moe_combine(yg, x2, tw_col, dest, g_final, tt, final_norm)


def _norm_body(x_ref, g_ref, o_ref):
    o_ref[...] = _rms(x_ref[...], g_ref[...])


def final_norm(x2, g, tm):
    T, D = x2.shape
    return pl.pallas_call(
        _norm_body,
        grid=(T // tm,),
        in_specs=[pl.BlockSpec((tm, D), lambda i: (i, 0)), pl.BlockSpec((1, D), lambda i: (0, 0))],
        out_specs=pl.BlockSpec((tm, D), lambda i: (i, 0)),
        out_shape=jax.ShapeDtypeStruct((T, D), F32),
        compiler_params=_cparams(1),
        name="final_norm",
    )(x2, g)


def _split_w_in(w):
    lo, hi = COL_SQ, COL_SQ + BA_COLS
    w_main = jnp.concatenate([w[:, :, :lo], w[:, :, hi:]], axis=2).astype(BF16)
    w_ba = jnp.pad(w[:, :, lo:hi], ((0, 0), (0, 0), (0, LANES - BA_COLS))).astype(BF16)
    return w_main, w_ba


def _pad_lanes(vec, offset):
    return jnp.pad(vec.astype(F32), (offset, LANES - offset - vec.shape[0])).reshape(1, LANES)


def mixer_layer(x2, batch, seq, layer, ln_g, w_main, w_ba, conv_w, a_log, dt_bias, gdn_norm_g, bias_tbl,
                sink_tbl, w_proj_gdn, w_proj_swa, w_out):
    proj, ba = in_proj(x2, ln_g.reshape(1, -1), w_main, w_ba, layer, tm=1024, tn=1280)
    qn, kn, vv, gb = gdn_prep(proj, ba, conv_w, _pad_lanes(a_log, GDN_V_HEADS),
                              _pad_lanes(dt_bias, GDN_V_HEADS), seq, ts=256)
    n_chunks = seq // GDN_CHUNK
    gcr = gb[:, GDN_V_HEADS:2 * GDN_V_HEADS].reshape(batch, n_chunks, GDN_CHUNK, GDN_QK_HEADS, 2)
    gcr = gcr.transpose(0, 3, 1, 4, 2).reshape(batch, GDN_QK_HEADS, n_chunks, 2 * GDN_CHUNK)
    o_a = gdn_core(qn, kn, vv, proj, gb, gcr, gdn_norm_g.reshape(1, -1).astype(F32), batch, seq, sb=512)
    o_b = swa_core(proj, bias_tbl, sink_tbl, batch, seq)
    y = merge_proj(o_a, o_b, w_proj_gdn, w_proj_swa, proj, layer, tm=1024, tn=512)
    return out_proj(y, w_out, x2, layer, tm=min(2048, x2.shape[0]), tn=512)


def kernel(x, ln_mix_g, w_in, conv_w, a_log, dt_bias, gdn_norm_g, sinks, rel_bias, w_proj_gdn, w_proj_swa,
           w_out, ln_ffn_g, ffn_w_gate, ffn_w_up, ffn_w_down, router_w, moe_w_gate, moe_w_up, moe_w_down,
           ln_final_g):
    B, S, D = x.shape
    depth = w_in.shape[0]
    x2 = x.reshape(B * S, D)
    g_final = ln_final_g.reshape(1, D)
    w_main, w_ba = _split_w_in(w_in)
    w_proj_gdn, w_proj_swa, w_out = w_proj_gdn.astype(BF16), w_proj_swa.astype(BF16), w_out.astype(BF16)
    normed = False
    for i in range(depth):
        bias_tbl, sink_tbl = swa_tables(rel_bias, sinks[i])
        x2 = mixer_layer(x2, B, S, i, ln_mix_g[i], w_main, w_ba, conv_w[i], a_log[i], dt_bias[i],
                         gdn_norm_g[i], bias_tbl, sink_tbl, w_proj_gdn, w_proj_swa, w_out)
        g_ffn = ln_ffn_g[i].reshape(1, D)
        j = i // 2
        if i % 2 == 0:
            f = ffn_w_gate.shape[2]
            fpad = (-f) % 512
            wg = jnp.concatenate([ffn_w_gate[j].astype(BF16), jnp.zeros((D, fpad), BF16)], axis=1)
            wu = jnp.concatenate([ffn_w_up[j].astype(BF16), jnp.zeros((D, fpad), BF16)], axis=1)
            wd = jnp.concatenate([ffn_w_down[j].astype(BF16), jnp.zeros((fpad, D), BF16)], axis=0)
            x2 = dense_ffn(x2, g_ffn, wg, wu, wd, tm=1024, tf=512)
        else:
            last = i == depth - 1
            x2 = moe_block(x2, g_ffn, router_w[j], moe_w_gate[j], moe_w_up[j], moe_w_down[j],
                           g_final, last, tm_r=512, tm_e=768, part=256, tf=512, tt=256)
            normed = last
    if not normed:
        x2 = final_norm(x2, g_final, tm=512)
    return x2.reshape(B, S, D)
```

```python
import functools
import math

import jax
import jax.numpy as jnp
from jax import lax
from jax.experimental import pallas as pl
from jax.experimental.pallas import tpu as pltpu

F32 = jnp.float32
BF16 = jnp.bfloat16

D_MODEL = 2048
GDN_QK_HEADS = 8
GDN_V_HEADS = 16
GDN_HEAD_DIM = 128
GDN_CONV = 4
GDN_CHUNK = 64
SWA_Q_HEADS = 32
SWA_KV_HEADS = 4
SWA_HEAD_DIM = 64
SWA_WINDOW = 128
REL_BUCKETS = 32
REL_MAX_DIST = 128
N_EXPERTS = 8
TOP_K = 2
NORM_EPS = 1e-6
GDN_KEY_DIM = GDN_QK_HEADS * GDN_HEAD_DIM
GDN_VAL_DIM = GDN_V_HEADS * GDN_HEAD_DIM
SWA_Q_DIM = SWA_Q_HEADS * SWA_HEAD_DIM
SWA_KV_DIM = SWA_KV_HEADS * SWA_HEAD_DIM

LANES = 128
VMEM_LIMIT = 56 * 1024 * 1024

COL_Q = 0
COL_K = COL_Q + GDN_KEY_DIM
COL_V = COL_K + GDN_KEY_DIM
COL_Z = COL_V + GDN_VAL_DIM
COL_SQ = COL_Z + GDN_VAL_DIM
COL_SK = COL_SQ + SWA_Q_DIM
COL_SV = COL_SK + SWA_KV_DIM
COL_GA = COL_SV + SWA_KV_DIM
COL_GB = COL_GA + D_MODEL
PROJ_COLS = COL_GB + D_MODEL
BA_COLS = 2 * GDN_V_HEADS
GDN_CONV_DIM = 2 * GDN_KEY_DIM + GDN_VAL_DIM

MASKED = -1e30
DMA_UNROLL = 8


def _cparams(n_axes):
    return pltpu.CompilerParams(dimension_semantics=("arbitrary",) * n_axes,
                                vmem_limit_bytes=VMEM_LIMIT)


def _rms(x, g):
    return x * lax.rsqrt(jnp.mean(x * x, axis=-1, keepdims=True) + NORM_EPS) * g


def _silu(x):
    return x * jax.nn.sigmoid(x)


def _dot(a, b):
    return jnp.dot(a, b, preferred_element_type=F32)


def _dot_nt(a, b):
    return lax.dot_general(a, b, (((1,), (1,)), ((), ())), preferred_element_type=F32)


def _dot_tn(a, b):
    return lax.dot_general(a, b, (((0,), (0,)), ((), ())), preferred_element_type=F32)


def _inproj_body(x_ref, g_ref, w_ref, wba_ref, o_ref, oba_ref, h_ref):
    @pl.when(pl.program_id(1) == 0)
    def _():
        h = _rms(x_ref[...], g_ref[...]).astype(BF16)
        h_ref[...] = h
        oba_ref[...] = _dot(h, wba_ref[...])

    o_ref[...] = _dot(h_ref[...], w_ref[...]).astype(o_ref.dtype)


def in_proj(x2, g, w_main, w_ba, layer, tm, tn):
    T, D = x2.shape
    N = w_main.shape[2]
    return pl.pallas_call(
        _inproj_body,
        grid=(T // tm, N // tn),
        in_specs=[pl.BlockSpec((tm, D), lambda i, j: (i, 0)),
                  pl.BlockSpec((1, D), lambda i, j: (0, 0)),
                  pl.BlockSpec((None, D, tn), lambda i, j: (layer, 0, j)),
                  pl.BlockSpec((None, D, LANES), lambda i, j: (layer, 0, 0))],
        out_specs=[pl.BlockSpec((tm, tn), lambda i, j: (i, j)),
                   pl.BlockSpec((tm, LANES), lambda i, j: (i, 0))],
        out_shape=[jax.ShapeDtypeStruct((T, N), BF16), jax.ShapeDtypeStruct((T, LANES), F32)],
        scratch_shapes=[pltpu.VMEM((tm, D), BF16)],
        compiler_params=_cparams(2),
        name="in_proj",
    )(x2, g, w_main, w_ba)


def _prep_body(x_ref, halo_ref, cw_ref, ba_ref, alog_ref, dtb_ref, q_ref, k_ref, v_ref, gb_ref,
               *, ts, seq, n_qk, n_v, chunk):
    first = (pl.program_id(0) * ts) % seq == 0
    for c in range(2 * n_qk + n_v):
        cs = slice(c * LANES, (c + 1) * LANES)
        cur = x_ref[:, cs].astype(F32)
        hal = halo_ref[8:16, cs].astype(F32)
        hal = jnp.where(first, 0.0, hal)
        ext = jnp.concatenate([hal, cur], axis=0)
        y = ext[5:5 + ts] * cw_ref[0:1, cs]
        y = y + ext[6:6 + ts] * cw_ref[1:2, cs]
        y = y + ext[7:7 + ts] * cw_ref[2:3, cs]
        y = y + cur * cw_ref[3:4, cs]
        y = _silu(y)
        if c < 2 * n_qk:
            y = y * lax.rsqrt(jnp.sum(y * y, axis=-1, keepdims=True) + 1e-6)
        if c < n_qk:
            q_ref[:, cs] = (y * (GDN_HEAD_DIM ** -0.5)).astype(q_ref.dtype)
        elif c < 2 * n_qk:
            k_ref[:, (c - n_qk) * LANES:(c - n_qk + 1) * LANES] = y.astype(k_ref.dtype)
        else:
            v_ref[:, (c - 2 * n_qk) * LANES:(c - 2 * n_qk + 1) * LANES] = y.astype(v_ref.dtype)

    ba = ba_ref[...]
    lane = lax.broadcasted_iota(jnp.int32, ba.shape, 1)
    pos = lax.broadcasted_iota(jnp.int32, ba.shape, 0) % chunk
    beta = jax.nn.sigmoid(ba)
    t = ba + dtb_ref[...]
    softplus = jnp.maximum(t, 0.0) + jnp.log1p(jnp.exp(-jnp.abs(t)))
    gc = -jnp.exp(alog_ref[...]) * softplus
    s = 1
    while s < chunk:
        gc = gc + jnp.where(pos >= s, pltpu.roll(gc, s, 0), 0.0)
        s *= 2
    gb_ref[...] = jnp.where(lane < n_v, beta, jnp.where(lane < 2 * n_v, gc, 0.0))


def gdn_prep(proj, ba, conv_w, alog_pad, dtb_pad, seq, ts):
    T = proj.shape[0]
    n_qk, n_v = GDN_QK_HEADS, GDN_V_HEADS
    body = functools.partial(_prep_body, ts=ts, seq=seq, n_qk=n_qk, n_v=n_v, chunk=GDN_CHUNK)
    return pl.pallas_call(
        body,
        grid=(T // ts,),
        in_specs=[pl.BlockSpec((ts, GDN_CONV_DIM), lambda i: (i, 0)),
                  pl.BlockSpec((16, GDN_CONV_DIM), lambda i: (jnp.maximum(i * (ts // 16) - 1, 0), 0)),
                  pl.BlockSpec((GDN_CONV, GDN_CONV_DIM), lambda i: (0, 0)),
                  pl.BlockSpec((ts, LANES), lambda i: (i, 0)),
                  pl.BlockSpec((1, LANES), lambda i: (0, 0)),
                  pl.BlockSpec((1, LANES), lambda i: (0, 0))],
        out_specs=[pl.BlockSpec((ts, GDN_KEY_DIM), lambda i: (i, 0)),
                   pl.BlockSpec((ts, GDN_KEY_DIM), lambda i: (i, 0)),
                   pl.BlockSpec((ts, GDN_VAL_DIM), lambda i: (i, 0)),
                   pl.BlockSpec((ts, LANES), lambda i: (i, 0))],
        out_shape=[jax.ShapeDtypeStruct((T, GDN_KEY_DIM), BF16),
                   jax.ShapeDtypeStruct((T, GDN_KEY_DIM), BF16),
                   jax.ShapeDtypeStruct((T, GDN_VAL_DIM), BF16),
                   jax.ShapeDtypeStruct((T, LANES), F32)],
        compiler_params=_cparams(1),
        name="gdn_prep",
    )(proj, proj, conv_w, ba, alog_pad, dtb_pad)


def _gdn_body(q_ref, k_ref, v_ref, z_ref, gb_ref, gcr_ref, gn_ref, o_ref,
              u_s, wq_s, qkd_s, gl_s, st_s, *, n_chunks, n_qk, n_v):
    C = GDN_CHUNK
    hd = GDN_HEAD_DIM
    assert n_v == 2 * n_qk and 2 * C == LANES and hd == LANES

    @pl.when(pl.program_id(1) == 0)
    def _():
        st_s[...] = jnp.zeros_like(st_s)

    low = lax.broadcasted_iota(jnp.int32, (1, LANES), 1) < C
    row = lax.broadcasted_iota(jnp.int32, (C, LANES), 0)
    col = jnp.bitwise_and(lax.broadcasted_iota(jnp.int32, (C, LANES), 1), C - 1)
    causal = row >= col
    strict = row > col
    zero_blk = jnp.zeros((C, hd), BF16)

    hs = range(n_qk)
    hvs = range(n_v)

    def block_diag(x):
        z = jnp.zeros_like(x)
        return jnp.concatenate([jnp.where(low, x, z), jnp.where(low, z, x)], axis=0)

    def phase1(c, carry):
        rows = pl.ds(pl.multiple_of(c * C, C), C)
        rows2 = pl.ds(pl.multiple_of(c * 2 * C, 2 * C), C)
        rows2b = pl.ds(pl.multiple_of(c * 2 * C, 2 * C) + C, C)
        gbc = gb_ref[rows, :]
        q = [q_ref[rows, h * hd:(h + 1) * hd] for h in hs]
        k = [k_ref[rows, h * hd:(h + 1) * hd] for h in hs]
        kdup = [jnp.concatenate([k[h], k[h]], axis=0) for h in hs]
        kk = [_dot_nt(k[h], kdup[h]) for h in hs]
        qk = [_dot_nt(q[h], kdup[h]) for h in hs]
        bcol1 = [gbc[:, hv:hv + 1] for hv in hvs]
        gcol1 = [gbc[:, n_v + hv:n_v + hv + 1] for hv in hvs]
        glast = [gbc[C - 1:C, n_v + hv:n_v + hv + 1] for hv in hvs]
        bcol = [jnp.where(low, bcol1[2 * h], bcol1[2 * h + 1]) for h in hs]
        gcol = [jnp.where(low, gcol1[2 * h], gcol1[2 * h + 1]) for h in hs]
        decay = [jnp.exp(jnp.where(causal, gcol[h] - gcr_ref[0, h, pl.ds(c, 1), :], -jnp.inf)) for h in hs]
        x = [jnp.where(strict, kk[h] * bcol[h] * decay[h], 0.0) * -1.0 for h in hs]
        nn = list(x)
        xb = [v.astype(BF16) for v in x]
        x = [_dot(xb[h], block_diag(xb[h])) for h in hs]
        egc = [jnp.exp(g) for g in gcol1]
        for h in hs:
            qf = q[h].astype(F32)
            kf = k[h].astype(F32)
            kd = []
            for hv in (2 * h, 2 * h + 1):
                wq_s[hv, rows2b, :] = (qf * egc[hv]).astype(BF16)
                kd.append(kf * jnp.exp(glast[hv] - gcol1[hv]))
                gl_s[hv, pl.ds(c, 1), :] = jnp.broadcast_to(jnp.exp(glast[hv]), (1, LANES))
            qkd_s[h, c, 0:C, :] = (qk[h] * decay[h]).astype(BF16)
            qkd_s[h, c, C:3 * C, :] = jnp.concatenate(kd, axis=0).T.astype(BF16)
        p = 2
        while 2 * p < C:
            xb = [v.astype(BF16) for v in x]
            res = [_dot(jnp.concatenate([xb[h], nn[h].astype(BF16)], axis=0), block_diag(xb[h])) for h in hs]
            nn = [nn[h] + x[h] + res[h][C:2 * C] for h in hs]
            x = [res[h][0:C] for h in hs]
            p *= 2
        nn = [nn[h] + x[h] + _dot(nn[h].astype(BF16), block_diag(x[h].astype(BF16))) for h in hs]
        nb = [n.astype(BF16) for n in nn]
        vb = [v_ref[rows, hv * hd:(hv + 1) * hd].astype(F32) * bcol1[hv] for hv in hvs]
        kbg = [k[hv // 2].astype(F32) * (bcol1[hv] * egc[hv]) for hv in hvs]
        z = zero_blk
        uw = [_dot(nb[h], jnp.concatenate([
            jnp.concatenate([vb[2 * h].astype(BF16), z, kbg[2 * h].astype(BF16), z], axis=1),
            jnp.concatenate([z, vb[2 * h + 1].astype(BF16), z, kbg[2 * h + 1].astype(BF16)], axis=1)], axis=0))
            for h in hs]
        for hv in hvs:
            h, j = hv // 2, hv % 2
            u_s[hv, rows, :] = vb[hv] + uw[h][:, j * hd:(j + 1) * hd]
            wq_s[hv, rows2, :] = (kbg[hv] + uw[h][:, (2 + j) * hd:(3 + j) * hd]).astype(BF16)
        return carry

    lax.fori_loop(0, n_chunks, phase1, 0)

    def phase2(c, carry):
        rows = pl.ds(pl.multiple_of(c * C, C), C)
        rows_wq = pl.ds(pl.multiple_of(c * 2 * C, 2 * C), 2 * C)
        sb = [st_s[hv].astype(BF16) for hv in hvs]
        ws = [_dot(wq_s[hv, rows_wq, :], sb[hv]) for hv in hvs]
        vnb = [(u_s[hv, rows, :] - ws[hv][0:C]).astype(BF16) for hv in hvs]
        z = zero_blk
        ov = [_dot(qkd_s[h, c], jnp.concatenate([jnp.concatenate([vnb[2 * h], z], axis=1),
                                                 jnp.concatenate([z, vnb[2 * h + 1]], axis=1)], axis=0))
              for h in hs]
        for hv in hvs:
            h, j = hv // 2, hv % 2
            st_s[hv] = st_s[hv] * gl_s[hv, pl.ds(c, 1), :] + ov[h][C:3 * C, j * hd:(j + 1) * hd]
        for hv in hvs:
            h, j = hv // 2, hv % 2
            o = ws[hv][C:2 * C] + ov[h][0:C, j * hd:(j + 1) * hd]
            o = o * lax.rsqrt(jnp.mean(o * o, axis=-1, keepdims=True) + NORM_EPS) * gn_ref[...]
            o = o * _silu(z_ref[rows, hv * hd:(hv + 1) * hd].astype(F32))
            o_ref[rows, hv * hd:(hv + 1) * hd] = o.astype(o_ref.dtype)
        return carry

    lax.fori_loop(0, n_chunks, phase2, 0)


def gdn_core(qn, kn, vv, proj, gb, gcr, gnorm, batch, seq, sb):
    T = qn.shape[0]
    n_qk, n_v, hd, C = GDN_QK_HEADS, GDN_V_HEADS, GDN_HEAD_DIM, GDN_CHUNK
    nc = sb // C
    nsb = seq // sb
    zc = COL_Z // (n_v * hd)
    body = functools.partial(_gdn_body, n_chunks=nc, n_qk=n_qk, n_v=n_v)
    blk = lambda b, s: (b * nsb + s, 0)
    return pl.pallas_call(
        body,
        grid=(batch, nsb),
        in_specs=[pl.BlockSpec((sb, n_qk * hd), blk),
                  pl.BlockSpec((sb, n_qk * hd), blk),
                  pl.BlockSpec((sb, n_v * hd), blk),
                  pl.BlockSpec((sb, n_v * hd), lambda b, s: (b * nsb + s, zc)),
                  pl.BlockSpec((sb, LANES), blk),
                  pl.BlockSpec((1, n_qk, nc, 2 * C), lambda b, s: (b, 0, s, 0)),
                  pl.BlockSpec((1, hd), lambda b, s: (0, 0))],
        out_specs=pl.BlockSpec((sb, n_v * hd), blk),
        out_shape=jax.ShapeDtypeStruct((T, n_v * hd), BF16),
        scratch_shapes=[pltpu.VMEM((n_v, sb, hd), F32),
                        pltpu.VMEM((n_v, 2 * sb, hd), BF16),
                        pltpu.VMEM((n_qk, nc, 3 * C, 2 * C), BF16),
                        pltpu.VMEM((n_v, nc, LANES), F32),
                        pltpu.VMEM((n_v, hd, hd), F32)],
        compiler_params=_cparams(2),
        name="gdn_core",
    )(qn, kn, vv, proj, gb, gcr, gnorm)


def _swa_body(q_ref, kp_ref, kc_ref, vp_ref, vc_ref, bias_ref, sink_ref, o_ref):
    W, dh = SWA_WINDOW, SWA_HEAD_DIM
    pairs = (SWA_Q_HEADS // SWA_KV_HEADS) // 2
    kvs = range(SWA_KV_HEADS)
    zero = jnp.zeros((2 * W, dh), BF16)
    one = jnp.ones((2 * W, dh), BF16)
    low = lax.broadcasted_iota(jnp.int32, (1, LANES), 1) < dh

    logits = []
    for hk in kvs:
        ks = slice(hk * dh, (hk + 1) * dh)
        kb = jnp.concatenate([kp_ref[:, ks], kc_ref[:, ks]], axis=0)
        krhs = jnp.concatenate([jnp.concatenate([kb, zero], axis=1),
                                jnp.concatenate([zero, kb], axis=1)], axis=0)
        qs = jnp.concatenate([q_ref[:, (hk * pairs + p) * LANES:(hk * pairs + p + 1) * LANES]
                              for p in range(pairs)], axis=0)
        qs = qs * jnp.asarray(dh ** -0.5, BF16)
        logits.append(_dot_nt(qs, krhs) + bias_ref[0, hk])

    outs, sinks = [], []
    for hk in kvs:
        ks = slice(hk * dh, (hk + 1) * dh)
        vb = jnp.concatenate([vp_ref[:, ks], vc_ref[:, ks]], axis=0)
        vrhs = jnp.concatenate([jnp.concatenate([vb, zero, one, zero], axis=1),
                                jnp.concatenate([zero, vb, zero, one], axis=1)], axis=0)
        ps, ms = [], []
        for c in range(2):
            lg = logits[hk][:, c * 2 * W:(c + 1) * 2 * W]
            sk = sink_ref[hk, c]
            m = jnp.max(jnp.maximum(jnp.maximum(lg[:, :W], lg[:, W:]), sk), axis=-1, keepdims=True)
            ps.append(jnp.exp(lg - m).astype(BF16))
            ms.append(sk - m)
        outs.append(_dot(jnp.concatenate(ps, axis=1), vrhs))
        sinks.append(jnp.exp(jnp.where(low, ms[0], ms[1])))

    for hk in kvs:
        out = outs[hk][:, :LANES] / (outs[hk][:, LANES:] + sinks[hk])
        for p in range(pairs):
            o_ref[:, (hk * pairs + p) * LANES:(hk * pairs + p + 1) * LANES] = (
                out[p * W:(p + 1) * W].astype(o_ref.dtype))


def swa_core(proj, bias_tbl, sink_tbl, batch, seq):
    T = proj.shape[0]
    W = SWA_WINDOW
    nb = seq // W
    qc, kc, vc = COL_SQ // SWA_Q_DIM, COL_SK // SWA_KV_DIM, COL_SV // SWA_KV_DIM
    cur = lambda b, n: b * nb + n
    prev = lambda b, n: jnp.maximum(b * nb + n - 1, 0)
    return pl.pallas_call(
        _swa_body,
        grid=(batch, nb),
        in_specs=[pl.BlockSpec((W, SWA_Q_DIM), lambda b, n: (cur(b, n), qc)),
                  pl.BlockSpec((W, SWA_KV_DIM), lambda b, n: (prev(b, n), kc)),
                  pl.BlockSpec((W, SWA_KV_DIM), lambda b, n: (cur(b, n), kc)),
                  pl.BlockSpec((W, SWA_KV_DIM), lambda b, n: (prev(b, n), vc)),
                  pl.BlockSpec((W, SWA_KV_DIM), lambda b, n: (cur(b, n), vc)),
                  pl.BlockSpec((1,) + bias_tbl.shape[1:], lambda b, n: (jnp.minimum(n, 1), 0, 0, 0)),
                  pl.BlockSpec(sink_tbl.shape, lambda b, n: (0, 0, 0, 0))],
        out_specs=pl.BlockSpec((W, SWA_Q_DIM), lambda b, n: (cur(b, n), 0)),
        out_shape=jax.ShapeDtypeStruct((T, SWA_Q_DIM), BF16),
        compiler_params=_cparams(2),
        name="swa_core",
    )(proj, proj, proj, proj, proj, bias_tbl, sink_tbl)


def _t5_bucket(dist):
    max_exact = REL_BUCKETS // 2
    large = max_exact + (jnp.log(jnp.maximum(dist, 1).astype(F32) / max_exact)
                         / math.log(REL_MAX_DIST / max_exact) * (REL_BUCKETS - max_exact)).astype(jnp.int32)
    large = jnp.minimum(large, REL_BUCKETS - 1)
    return jnp.where(dist < max_exact, dist, large)


def swa_tables(rel_bias, sinks):
    W, hq, hkv = SWA_WINDOW, SWA_Q_HEADS, SWA_KV_HEADS
    pairs = (hq // hkv) // 2
    by_dist = rel_bias.astype(F32)[_t5_bucket(jnp.arange(W))]
    period = 3 * W
    vec = jnp.concatenate([jnp.full((1, hq), MASKED, F32), by_dist[::-1],
                           jnp.full((period - W - 1, hq), MASKED, F32)], axis=0).T
    tbl = jnp.tile(vec, (1, W))[:, :W * (period - 1)].reshape(hq, W, period - 1)[:, :, :2 * W]
    first = jnp.where(jnp.arange(2 * W)[None, None, :] < W, MASKED, tbl)
    tbl = jnp.stack([first, tbl])
    tbl = tbl.reshape(2, hkv, pairs, 2, W, 2 * W).transpose(0, 1, 2, 4, 3, 5)
    tbl = tbl.reshape(2, hkv, pairs * W, 4 * W)
    sk = sinks.astype(F32).reshape(hkv, pairs, 2).transpose(0, 2, 1)
    sk = jnp.broadcast_to(sk[:, :, :, None, None], (hkv, 2, pairs, W, LANES)).reshape(hkv, 2, pairs * W, LANES)
    return tbl, sk


def _merge_body(oa_ref, ob_ref, wa_ref, wb_ref, ga_ref, gb_ref, y_ref):
    a = _dot(oa_ref[...], wa_ref[...])
    b = _dot(ob_ref[...], wb_ref[...])
    y = jax.nn.sigmoid(ga_ref[...].astype(F32)) * a + jax.nn.sigmoid(gb_ref[...].astype(F32)) * b
    y_ref[...] = y.astype(y_ref.dtype)


def merge_proj(o_a, o_b, w_a, w_b, proj, layer, tm, tn):
    T, K = o_a.shape
    N = w_a.shape[2]
    ga0, gb0 = COL_GA // tn, COL_GB // tn
    return pl.pallas_call(
        _merge_body,
        grid=(T // tm, N // tn),
        in_specs=[pl.BlockSpec((tm, K), lambda i, j: (i, 0)),
                  pl.BlockSpec((tm, K), lambda i, j: (i, 0)),
                  pl.BlockSpec((None, K, tn), lambda i, j: (layer, 0, j)),
                  pl.BlockSpec((None, K, tn), lambda i, j: (layer, 0, j)),
                  pl.BlockSpec((tm, tn), lambda i, j: (i, ga0 + j)),
                  pl.BlockSpec((tm, tn), lambda i, j: (i, gb0 + j))],
        out_specs=pl.BlockSpec((tm, tn), lambda i, j: (i, j)),
        out_shape=jax.ShapeDtypeStruct((T, N), BF16),
        compiler_params=_cparams(2),
        name="merge_proj",
    )(o_a, o_b, w_a, w_b, proj, proj)


def _resid_body(y_ref, w_ref, x_ref, o_ref):
    o_ref[...] = x_ref[...] + _dot(y_ref[...], w_ref[...])


def out_proj(y, w, x2, layer, tm, tn):
    T, K = y.shape
    N = w.shape[2]
    return pl.pallas_call(
        _resid_body,
        grid=(T // tm, N // tn),
        in_specs=[pl.BlockSpec((tm, K), lambda i, j: (i, 0)),
                  pl.BlockSpec((None, K, tn), lambda i, j: (layer, 0, j)),
                  pl.BlockSpec((tm, tn), lambda i, j: (i, j))],
        out_specs=pl.BlockSpec((tm, tn), lambda i, j: (i, j)),
        out_shape=jax.ShapeDtypeStruct((T, N), F32),
        compiler_params=_cparams(2),
        name="out_proj",
    )(y, w, x2)


def _ffn_body(x_ref, g_ref, wg_ref, wu_ref, wd_ref, o_ref, h_ref):
    @pl.when(pl.program_id(1) == 0)
    def _():
        x = x_ref[...]
        h_ref[...] = _rms(x, g_ref[...]).astype(BF16)
        o_ref[...] = x

    h = h_ref[...]
    mid = (_silu(_dot(h, wg_ref[...])) * _dot(h, wu_ref[...])).astype(BF16)
    o_ref[...] += _dot(mid, wd_ref[...])


def dense_ffn(x2, g, w_gate, w_up, w_down, tm, tf):
    T, D = x2.shape
    F = w_gate.shape[1]
    return pl.pallas_call(
        _ffn_body,
        grid=(T // tm, F // tf),
        in_specs=[pl.BlockSpec((tm, D), lambda i, f: (i, 0)),
                  pl.BlockSpec((1, D), lambda i, f: (0, 0)),
                  pl.BlockSpec((D, tf), lambda i, f: (0, f)),
                  pl.BlockSpec((D, tf), lambda i, f: (0, f)),
                  pl.BlockSpec((tf, D), lambda i, f: (f, 0))],
        out_specs=pl.BlockSpec((tm, D), lambda i, f: (i, 0)),
        out_shape=jax.ShapeDtypeStruct((T, D), F32),
        scratch_shapes=[pltpu.VMEM((tm, D), BF16)],
        compiler_params=_cparams(2),
        name="dense_ffn",
    )(x2, g, w_gate, w_up, w_down)


def _router_body(x_ref, g_ref, rwt_ref, e_ref, w_ref):
    h = _rms(x_ref[...], g_ref[...]).astype(BF16)
    lt = _dot_nt(rwt_ref[...], h)
    idx = lax.broadcasted_iota(jnp.int32, lt.shape, 0)
    m1 = jnp.max(lt, axis=0, keepdims=True)
    i1 = jnp.min(jnp.where(lt == m1, idx, N_EXPERTS), axis=0, keepdims=True)
    lt2 = jnp.where(idx == i1, -jnp.inf, lt)
    m2 = jnp.max(lt2, axis=0, keepdims=True)
    i2 = jnp.min(jnp.where(lt2 == m2, idx, N_EXPERTS), axis=0, keepdims=True)
    e2 = jnp.exp(m2 - m1)
    den = 1.0 + e2
    e_ref[...] = jnp.concatenate([i1, i2], axis=0)
    w_ref[...] = jnp.concatenate([1.0 / den, e2 / den], axis=0)


def moe_router(x2, g, rw_t, tm):
    T, D = x2.shape
    return pl.pallas_call(
        _router_body,
        grid=(T // tm,),
        in_specs=[pl.BlockSpec((tm, D), lambda i: (i, 0)),
                  pl.BlockSpec((1, D), lambda i: (0, 0)),
                  pl.BlockSpec((N_EXPERTS, D), lambda i: (0, 0))],
        out_specs=[pl.BlockSpec((TOP_K, tm), lambda i: (0, i)),
                   pl.BlockSpec((TOP_K, tm), lambda i: (0, i))],
        out_shape=[jax.ShapeDtypeStruct((TOP_K, T), jnp.int32),
                   jax.ShapeDtypeStruct((TOP_K, T), F32)],
        compiler_params=_cparams(1),
        name="moe_router",
    )(x2, g, rw_t)


def _moe_body(be_ref, nv_ref, nu_ref, tok_ref, x_hbm, g_ref, wg_ref, wu_ref, wd_ref, o_ref,
              xbuf, h_ref, sem, *, tm, part, per_step):
    i = pl.program_id(0)
    f = pl.program_id(1)
    nf = pl.num_programs(1)
    n_used = nu_ref[0]
    n_buf = xbuf.shape[0]

    def row_copy(idx, r):
        return pltpu.make_async_copy(x_hbm.at[pl.ds(idx, 1), :], xbuf.at[pl.ds(r, 1), :], sem.at[0])

    def wait_all():
        for r in range(n_buf):
            row_copy(0, r).wait()

    def load_rows(rows):
        wait_all()
        h_ref[0:rows, :] = _rms(xbuf[0:rows, :], g_ref[...]).astype(BF16)

    def compute(rows):
        nxt = jnp.minimum(i + 1, n_used - 1)
        for u in range(per_step):
            r = f * per_step + u
            row_copy(tok_ref[nxt * tm + r], r).start()
        h = h_ref[0:rows, :]
        gate = _dot(h, wg_ref[...].astype(BF16))
        up = _dot(h, wu_ref[...].astype(BF16))
        mid = (_silu(gate) * up).astype(BF16)
        o_ref[0:rows, :] += _dot(mid, wd_ref[...].astype(BF16))

    active = i < n_used
    pieces = nv_ref[i]

    @pl.when((f == 0) & (i == 0))
    def _():
        def issue(r8, c):
            for u in range(DMA_UNROLL):
                r = r8 * DMA_UNROLL + u
                row_copy(tok_ref[r], r).start()
            return c
        lax.fori_loop(0, n_buf // DMA_UNROLL, issue, 0)

    for p in range(1, tm // part + 1):
        pl.when((f == 0) & active & (pieces == p))(functools.partial(load_rows, p * part))

    @pl.when(f == 0)
    def _():
        o_ref[...] = jnp.zeros_like(o_ref)

    for p in range(1, tm // part + 1):
        pl.when(active & (pieces == p))(functools.partial(compute, p * part))

    @pl.when((f == nf - 1) & (i == n_used - 1))
    def _():
        wait_all()


def moe_experts(x2, g, w_gate, w_up, w_down, block_e, n_valid, n_used, row_tok, tm, part, tf):
    T, D = x2.shape
    E, _, F = w_gate.shape
    n_rows = row_tok.shape[0]
    nf = F // tf
    assert tm % part == 0
    per_step = -(-tm // (nf * DMA_UNROLL)) * DMA_UNROLL
    n_buf = nf * per_step
    row_tok = jnp.pad(row_tok, (0, n_buf - tm))

    def fidx(i, f, nu):
        return jnp.where(i < nu[0], f, nf - 1)

    body = functools.partial(_moe_body, tm=tm, part=part, per_step=per_step)
    return pl.pallas_call(
        body,
        grid_spec=pltpu.PrefetchScalarGridSpec(
            num_scalar_prefetch=4,
            grid=(n_rows // tm, nf),
            in_specs=[pl.BlockSpec(memory_space=pl.ANY),
                      pl.BlockSpec((1, D), lambda i, f, be, nv, nu, tok: (0, 0)),
                      pl.BlockSpec((None, D, tf), lambda i, f, be, nv, nu, tok: (be[i], 0, fidx(i, f, nu))),
                      pl.BlockSpec((None, D, tf), lambda i, f, be, nv, nu, tok: (be[i], 0, fidx(i, f, nu))),
                      pl.BlockSpec((None, tf, D), lambda i, f, be, nv, nu, tok: (be[i], fidx(i, f, nu), 0))],
            out_specs=pl.BlockSpec((tm, D), lambda i, f, be, nv, nu, tok: (i, 0)),
            scratch_shapes=[pltpu.VMEM((n_buf, D), F32),
                            pltpu.VMEM((tm, D), BF16),
                            pltpu.SemaphoreType.DMA((1,))]),
        out_shape=jax.ShapeDtypeStruct((n_rows, D), F32),
        compiler_params=_cparams(2),
        name="moe_experts",
    )(block_e, n_valid, n_used, row_tok, x2, g, w_gate, w_up, w_down)


def _combine_body(dest_ref, yg_hbm, x_ref, tw_ref, g_ref, o_ref, ybuf, sem, *, tt, n_steps, final_norm):
    i = pl.program_id(0)

    def gather(blk, slot):
        def issue(r8, c):
            for u in range(DMA_UNROLL):
                r = r8 * DMA_UNROLL + u
                for k in range(TOP_K):
                    pltpu.make_async_copy(yg_hbm.at[pl.ds(dest_ref[(blk * tt + r) * TOP_K + k], 1), :],
                                          ybuf.at[slot, k, pl.ds(r, 1), :], sem.at[slot]).start()
            return c
        lax.fori_loop(0, tt // DMA_UNROLL, issue, 0)

    @pl.when(i == 0)
    def _():
        gather(0, 0)

    slot = i % 2

    @pl.when(i + 1 < n_steps)
    def _():
        gather(i + 1, 1 - slot)

    for r in range(tt):
        for k in range(TOP_K):
            pltpu.make_async_copy(yg_hbm.at[pl.ds(0, 1), :], ybuf.at[slot, k, pl.ds(r, 1), :],
                                  sem.at[slot]).wait()

    tw = tw_ref[...]
    y = x_ref[...] + ybuf[slot, 0] * tw[:, 0:1] + ybuf[slot, 1] * tw[:, 1:2]
    if final_norm:
        y = _rms(y, g_ref[...])
    o_ref[...] = y


def moe_combine(yg, x2, tw_col, dest, g_final, tt, final_norm):
    T, D = x2.shape
    n_steps = T // tt
    body = functools.partial(_combine_body, tt=tt, n_steps=n_steps, final_norm=final_norm)
    return pl.pallas_call(
        body,
        grid_spec=pltpu.PrefetchScalarGridSpec(
            num_scalar_prefetch=1,
            grid=(n_steps,),
            in_specs=[pl.BlockSpec(memory_space=pl.ANY),
                      pl.BlockSpec((tt, D), lambda i, d: (i, 0)),
                      pl.BlockSpec((tt, LANES), lambda i, d: (i, 0)),
                      pl.BlockSpec((1, D), lambda i, d: (0, 0))],
            out_specs=pl.BlockSpec((tt, D), lambda i, d: (i, 0)),
            scratch_shapes=[pltpu.VMEM((2, TOP_K, tt, D), F32),
                            pltpu.SemaphoreType.DMA((2,))]),
        out_shape=jax.ShapeDtypeStruct((T, D), F32),
        compiler_params=_cparams(1),
        name="moe_combine",
    )(dest, yg, x2, tw_col, g_final)


def moe_plan(top_e, tm, part):
    T = top_e.shape[1]
    TK = T * TOP_K
    experts = jnp.arange(N_EXPERTS, dtype=jnp.int32)
    rows = TK // LANES
    onehot = (top_e.T.reshape(1, rows, LANES) == experts[:, None, None]).astype(F32)
    tri = (jnp.arange(LANES)[:, None] <= jnp.arange(LANES)[None, :]).astype(F32)
    within = jnp.einsum('erk,kl->erl', onehot, tri, precision=lax.Precision.HIGHEST)
    before = (jnp.arange(rows)[:, None] < jnp.arange(rows)[None, :]).astype(F32)
    row_off = jnp.einsum('er,rs->es', within[:, :, -1], before, precision=lax.Precision.HIGHEST)
    cum = within + row_off[:, :, None]
    counts = cum[:, -1, -1].astype(jnp.int32)
    padded = (counts + tm - 1) // tm * tm
    pad_end = jnp.cumsum(padded)
    pad_start = pad_end - padded
    dest = jnp.sum(onehot * (cum - 1.0 + pad_start.astype(F32)[:, None, None]), axis=0)
    dest = dest.reshape(TK).astype(jnp.int32)
    n_blocks = -(-(TK + N_EXPERTS * (tm - 1)) // tm)
    n_rows = n_blocks * tm
    row_tok = jnp.zeros((n_rows,), jnp.int32).at[dest].set(jnp.arange(TK, dtype=jnp.int32) // TOP_K)
    blk = jnp.arange(n_blocks, dtype=jnp.int32)
    blk_onehot = ((blk[:, None] * tm >= pad_start[None, :]) & (blk[:, None] * tm < pad_end[None, :])).astype(jnp.int32)
    n_used = (pad_end[-1:] // tm).astype(jnp.int32)
    last_e = jnp.max(jnp.where(padded > 0, experts, 0))
    block_e = jnp.where(blk < n_used[0], jnp.sum(blk_onehot * experts[None, :], axis=1), last_e)
    n_valid = jnp.clip(jnp.sum(blk_onehot * (counts + pad_start)[None, :], axis=1) - blk * tm, 0, tm)
    n_valid = (n_valid + part - 1) // part
    return dest.astype(jnp.int32), row_tok, block_e.astype(jnp.int32), n_valid.astype(jnp.int32), n_used


def moe_block(x2, g, router_w, w_gate, w_up, w_down, g_final, final_norm, tm_r, tm_e, part, tf, tt):
    top_e, top_w = moe_router(x2, g, router_w.T.astype(BF16), tm_r)
    dest, row_tok, block_e, n_valid, n_used = moe_plan(top_e, tm_e, part)
    yg = moe_experts(x2, g, w_gate, w_up, w_down, block_e, n_valid, n_used, row_tok, tm_e, part, tf)
    tw_col = jnp.pad(top_w.T, ((0, 0), (0, LANES - TOP_K)))
    return moe_combine(yg, x2, tw_col, dest, g_final, tt, final_norm)


def _norm_body(x_ref, g_ref, o_ref):
    o_ref[...] = _rms(x_ref[...], g_ref[...])


def final_norm(x2, g, tm):
    T, D = x2.shape
    return pl.pallas_call(
        _norm_body,
        grid=(T // tm,),
        in_specs=[pl.BlockSpec((tm, D), lambda i: (i, 0)), pl.BlockSpec((1, D), lambda i: (0, 0))],
        out_specs=pl.BlockSpec((tm, D), lambda i: (i, 0)),
        out_shape=jax.ShapeDtypeStruct((T, D), F32),
        compiler_params=_cparams(1),
        name="final_norm",
    )(x2, g)


def _split_w_in(w):
    lo, hi = COL_SQ, COL_SQ + BA_COLS
    w_main = jnp.concatenate([w[:, :, :lo], w[:, :, hi:]], axis=2).astype(BF16)
    w_ba = jnp.pad(w[:, :, lo:hi], ((0, 0), (0, 0), (0, LANES - BA_COLS))).astype(BF16)
    return w_main, w_ba


def _pad_lanes(vec, offset):
    return jnp.pad(vec.astype(F32), (offset, LANES - offset - vec.shape[0])).reshape(1, LANES)


def mixer_layer(x2, batch, seq, layer, ln_g, w_main, w_ba, conv_w, a_log, dt_bias, gdn_norm_g, bias_tbl,
                sink_tbl, w_proj_gdn, w_proj_swa, w_out):
    proj, ba = in_proj(x2, ln_g.reshape(1, -1), w_main, w_ba, layer, tm=1024, tn=1280)
    qn, kn, vv, gb = gdn_prep(proj, ba, conv_w, _pad_lanes(a_log, GDN_V_HEADS),
                              _pad_lanes(dt_bias, GDN_V_HEADS), seq, ts=256)
    n_chunks = seq // GDN_CHUNK
    gcr = gb[:, GDN_V_HEADS:2 * GDN_V_HEADS].reshape(batch, n_chunks, GDN_CHUNK, GDN_QK_HEADS, 2)
    gcr = gcr.transpose(0, 3, 1, 4, 2).reshape(batch, GDN_QK_HEADS, n_chunks, 2 * GDN_CHUNK)
    o_a = gdn_core(qn, kn, vv, proj, gb, gcr, gdn_norm_g.reshape(1, -1).astype(F32), batch, seq, sb=512)
    o_b = swa_core(proj, bias_tbl, sink_tbl, batch, seq)
    y = merge_proj(o_a, o_b, w_proj_gdn, w_proj_swa, proj, layer, tm=1024, tn=512)
    return out_proj(y, w_out, x2, layer, tm=min(2048, x2.shape[0]), tn=512)


def kernel(x, ln_mix_g, w_in, conv_w, a_log, dt_bias, gdn_norm_g, sinks, rel_bias, w_proj_gdn, w_proj_swa,
           w_out, ln_ffn_g, ffn_w_gate, ffn_w_up, ffn_w_down, router_w, moe_w_gate, moe_w_up, moe_w_down,
           ln_final_g):
    B, S, D = x.shape
    depth = w_in.shape[0]
    x2 = x.reshape(B * S, D)
    g_final = ln_final_g.reshape(1, D)
    w_main, w_ba = _split_w_in(w_in)
    w_proj_gdn, w_proj_swa, w_out = w_proj_gdn.astype(BF16), w_proj_swa.astype(BF16), w_out.astype(BF16)
    normed = False
    for i in range(depth):
        bias_tbl, sink_tbl = swa_tables(rel_bias, sinks[i])
        x2 = mixer_layer(x2, B, S, i, ln_mix_g[i], w_main, w_ba, conv_w[i], a_log[i], dt_bias[i],
                         gdn_norm_g[i], bias_tbl, sink_tbl, w_proj_gdn, w_proj_swa, w_out)
        g_ffn = ln_ffn_g[i].reshape(1, D)
        j = i // 2
        if i % 2 == 0:
            f = ffn_w_gate.shape[2]
            fpad = (-f) % 512
            wg = jnp.concatenate([ffn_w_gate[j].astype(BF16), jnp.zeros((D, fpad), BF16)], axis=1)
            wu = jnp.concatenate([ffn_w_up[j].astype(BF16), jnp.zeros((D, fpad), BF16)], axis=1)
            wd = jnp.concatenate([ffn_w_down[j].astype(BF16), jnp.zeros((fpad, D), BF16)], axis=0)
            x2 = dense_ffn(x2, g_ffn, wg, wu, wd, tm=1024, tf=512)
        else:
            last = i == depth - 1
            x2 = moe_block(x2, g_ffn, router_w[j], moe_w_gate[j], moe_w_up[j], moe_w_down[j],
                           g_final, last, tm_r=512, tm_e=768, part=256, tf=512, tt=256)
            normed = last
    if not normed:
        x2 = final_norm(x2, g_final, tm=512)
    return x2.reshape(B, S, D)
```
